```python
import math
import jax, jax.numpy as jnp
from jax import lax
import numpy as np

D_MODEL = 2048
BATCH = 4
SEQ = 8192
DEPTH = 1

MIX_WIDTH = D_MODEL
ATTN_WIDTH = D_MODEL // 2
HEAD_DIM = 128
N_HEADS = ATTN_WIDTH // HEAD_DIM
DILATION_PATTERNS = ((128, 1), (512, 4), (2048, 16))
SSM_WIDTH = MIX_WIDTH - ATTN_WIDTH
SSM_GROUP = 16
N_SSM_GROUPS = SSM_WIDTH // SSM_GROUP
STATE_DIM = 64
SSM_CHUNK = 128
IN_WIDTH = 3 * ATTN_WIDTH + SSM_WIDTH
D_FF = 4 * D_MODEL
N_MOD = 6
EPS = 1e-6
DT_MIN = 1e-3
DT_MAX = 1e-1

kernel_name = "hymba_dilated_attn_s5_sqrelu_adaln"


def rms_norm(x, g):
    xf = x.astype(jnp.float32)
    y = xf * lax.rsqrt(jnp.mean(xf * xf, axis=-1, keepdims=True) + EPS) * g.astype(jnp.float32)
    return y.astype(x.dtype)


def alibi_slopes(n_heads):
    return 2.0 ** (-8.0 * (jnp.arange(n_heads, dtype=jnp.float32) + 1.0) / n_heads)


def dilated_pattern(q, k, v, slopes, window, dilation):
    b, s, h, e = q.shape
    n = window // dilation
    L = s // dilation
    nb = -(-L // n)
    Lp = nb * n

    def to_blocks(t):
        t = t.reshape(b, L, dilation, h, e)
        t = jnp.pad(t, ((0, 0), (0, Lp - L), (0, 0), (0, 0), (0, 0)))
        return t.reshape(b, nb, n, dilation, h, e)

    def with_prev(t):
        prev = jnp.pad(t, ((0, 0), (1, 0), (0, 0), (0, 0), (0, 0), (0, 0)))[:, :-1]
        return jnp.concatenate([prev, t], axis=2)

    qb = to_blocks(q)
    kw = with_prev(to_blocks(k))
    vw = with_prev(to_blocks(v))
    scores = jnp.einsum('bnqrhe,bnkrhe->bnrhqk', qb, kw) * (HEAD_DIM ** -0.5)

    qi = jnp.arange(n)[:, None]
    ki = jnp.arange(2 * n)[None, :]
    steps = qi - ki + n
    key_idx = jnp.arange(nb)[:, None, None] * n - n + ki
    valid = (steps >= 0) & (steps <= n) & (key_idx >= 0)
    bias = -slopes[:, None, None] * (steps * dilation).astype(jnp.float32)
    scores = scores + bias[None, None, None]
    scores = jnp.where(valid[None, :, None, None], scores, -jnp.inf)

    m = jnp.max(scores, axis=-1, keepdims=True)
    p = jnp.exp(scores - m)
    denom = jnp.sum(p, axis=-1, keepdims=True)
    out = jnp.einsum('bnrhqk,bnkrhe->bnqrhe', p, vw)
    denom_q = jnp.moveaxis(denom[..., 0], -1, 2)
    lse_q = jnp.moveaxis(m[..., 0], -1, 2) + jnp.log(denom_q)
    out = out / denom_q[..., None]
    out = out.reshape(b, Lp, dilation, h, e)[:, :L].reshape(b, s, h, e)
    lse = lse_q.reshape(b, Lp, dilation, h)[:, :L].reshape(b, s, h)
    return out, lse


def dilated_attention(q, k, v):
    slopes = alibi_slopes(N_HEADS)
    outs, lses = [], []
    for window, dilation in DILATION_PATTERNS:
        o, l = dilated_pattern(q, k, v, slopes, window, dilation)
        outs.append(o)
        lses.append(l)
    w = jax.nn.softmax(jnp.stack(lses, axis=0), axis=0)
    return jnp.sum(w[..., None] * jnp.stack(outs, axis=0), axis=0)


def _ssm_combine(e_i, e_j):
    a_i, b_i = e_i
    a_j, b_j = e_j
    return a_j * a_i, a_j * b_i + b_j


def s5_mixer(u, lam_re, lam_im, log_step, b_re, b_im, c_re, c_im, d_skip):
    bsz, s, _ = u.shape
    uf = u.astype(jnp.float32).reshape(bsz, s, N_SSM_GROUPS, SSM_GROUP)
    lam = lax.complex(lam_re.astype(jnp.float32), lam_im.astype(jnp.float32))
    step = jnp.exp(log_step.astype(jnp.float32))[:, None]
    a_bar = jnp.exp(lam * step)
    b_mat = lax.complex(b_re.astype(jnp.float32), b_im.astype(jnp.float32))
    b_bar = ((a_bar - 1.0) / lam)[..., None] * b_mat
    c_mat = lax.complex(c_re.astype(jnp.float32), c_im.astype(jnp.float32))
    n_chunks = s // SSM_CHUNK
    u_chunks = uf.reshape(bsz, n_chunks, SSM_CHUNK, N_SSM_GROUPS, SSM_GROUP).transpose(1, 0, 2, 3, 4)
    a_full = jnp.broadcast_to(a_bar, (bsz, SSM_CHUNK, N_SSM_GROUPS, STATE_DIM))

    def segment(h, u_c):
        bu = jnp.einsum('blgi,gpi->blgp', u_c.astype(jnp.complex64), b_bar)
        bu = bu.at[:, 0].add(a_bar * h)
        _, hs = lax.associative_scan(_ssm_combine, (a_full, bu), axis=1)
        y = jnp.real(jnp.einsum('blgp,gip->blgi', hs, c_mat))
        return hs[:, -1], y

    h0 = jnp.zeros((bsz, N_SSM_GROUPS, STATE_DIM), jnp.complex64)
    _, ys = lax.scan(segment, h0, u_chunks)
    y = ys.transpose(1, 0, 2, 3, 4).reshape(bsz, s, N_SSM_GROUPS, SSM_GROUP)
    y = y + d_skip.astype(jnp.float32).reshape(N_SSM_GROUPS, SSM_GROUP) * uf
    return y.reshape(bsz, s, SSM_WIDTH)


def hybrid_layer(x, c, w_ada, b_ada, norm1_g, w_in, q_norm_g, k_norm_g, lam_re, lam_im,
                 log_step, b_re, b_im, c_re, c_im, d_skip, w_glu, b_glu, attn_out_g,
                 ssm_out_g, w_out, norm2_g, w_ff1, w_ff2):
    bsz, s, _ = x.shape
    mod = (jax.nn.silu(c) @ w_ada + b_ada)[:, None, :]
    sh1, sc1, g1, sh2, sc2, g2 = jnp.split(mod, N_MOD, axis=-1)

    h = rms_norm(x, norm1_g) * (1.0 + sc1) + sh1
    proj = h @ w_in
    q, k, v, u = jnp.split(proj, [ATTN_WIDTH, 2 * ATTN_WIDTH, 3 * ATTN_WIDTH], axis=-1)
    q = rms_norm(q.reshape(bsz, s, N_HEADS, HEAD_DIM), q_norm_g).astype(jnp.float32)
    k = rms_norm(k.reshape(bsz, s, N_HEADS, HEAD_DIM), k_norm_g).astype(jnp.float32)
    v = v.reshape(bsz, s, N_HEADS, HEAD_DIM).astype(jnp.float32)
    attn = dilated_attention(q, k, v).reshape(bsz, s, ATTN_WIDTH).astype(x.dtype)

    y = jax.nn.gelu(s5_mixer(u, lam_re, lam_im, log_step, b_re, b_im, c_re, c_im, d_skip)).astype(x.dtype)
    ssm = y * jax.nn.sigmoid(y @ w_glu + b_glu)

    mixed = jnp.concatenate([rms_norm(attn, attn_out_g), rms_norm(ssm, ssm_out_g)], axis=-1) @ w_out
    x = x + g1 * mixed

    h2 = rms_norm(x, norm2_g) * (1.0 + sc2) + sh2
    ff = jnp.square(jax.nn.relu(h2 @ w_ff1)) @ w_ff2
    return x + g2 * ff


def setup_inputs(seed: int = 0) -> dict:
    key = jax.random.key(seed)
    ks = jax.random.split(key, 26)
    f32 = jnp.float32
    nrm = lambda k, shape, scale: jax.random.normal(k, shape, f32) * scale
    G, P = N_SSM_GROUPS, STATE_DIM
    lam_im_base = jnp.pi * jnp.arange(P, dtype=f32)
    return {
        "x": nrm(ks[0], (BATCH, SEQ, D_MODEL), 1.0),
        "c": nrm(ks[1], (BATCH, D_MODEL), 1.0),
        "w_ada": nrm(ks[2], (DEPTH, D_MODEL, N_MOD * D_MODEL), 0.5 * D_MODEL ** -0.5),
        "b_ada": nrm(ks[3], (DEPTH, N_MOD * D_MODEL), 0.01),
        "norm1_g": 1.0 + nrm(ks[4], (DEPTH, D_MODEL), 0.01),
        "w_in": nrm(ks[5], (DEPTH, D_MODEL, IN_WIDTH), D_MODEL ** -0.5),
        "q_norm_g": 1.0 + nrm(ks[6], (DEPTH, HEAD_DIM), 0.01),
        "k_norm_g": 1.0 + nrm(ks[7], (DEPTH, HEAD_DIM), 0.01),
        "lam_re": -0.5 + nrm(ks[8], (DEPTH, G, P), 0.01),
        "lam_im": lam_im_base + nrm(ks[9], (DEPTH, G, P), 0.01),
        "log_step": jax.random.uniform(ks[10], (DEPTH, G), f32, math.log(DT_MIN), math.log(DT_MAX)),
        "b_re": nrm(ks[11], (DEPTH, G, P, SSM_GROUP), (2.0 * SSM_GROUP) ** -0.5),
        "b_im": nrm(ks[12], (DEPTH, G, P, SSM_GROUP), (2.0 * SSM_GROUP) ** -0.5),
        "c_re": nrm(ks[13], (DEPTH, G, SSM_GROUP, P), (2.0 * P) ** -0.5),
        "c_im": nrm(ks[14], (DEPTH, G, SSM_GROUP, P), (2.0 * P) ** -0.5),
        "d_skip": nrm(ks[15], (DEPTH, SSM_WIDTH), 1.0),
        "w_glu": nrm(ks[16], (DEPTH, SSM_WIDTH, SSM_WIDTH), SSM_WIDTH ** -0.5),
        "b_glu": nrm(ks[17], (DEPTH, SSM_WIDTH), 0.01),
        "attn_out_g": 1.0 + nrm(ks[18], (DEPTH, ATTN_WIDTH), 0.01),
        "ssm_out_g": 1.0 + nrm(ks[19], (DEPTH, SSM_WIDTH), 0.01),
        "w_out": nrm(ks[20], (DEPTH, MIX_WIDTH, D_MODEL), MIX_WIDTH ** -0.5),
        "norm2_g": 1.0 + nrm(ks[21], (DEPTH, D_MODEL), 0.01),
        "w_ff1": nrm(ks[22], (DEPTH, D_MODEL, D_FF), D_MODEL ** -0.5),
        "w_ff2": nrm(ks[23], (DEPTH, D_FF, D_MODEL), D_FF ** -0.5),
    }


def reference(x, c, w_ada, b_ada, norm1_g, w_in, q_norm_g, k_norm_g, lam_re, lam_im, log_step,
              b_re, b_im, c_re, c_im, d_skip, w_glu, b_glu, attn_out_g, ssm_out_g, w_out,
              norm2_g, w_ff1, w_ff2):
    for l in range(DEPTH):
        x = hybrid_layer(x, c, w_ada[l], b_ada[l], norm1_g[l], w_in[l], q_norm_g[l], k_norm_g[l],
                         lam_re[l], lam_im[l], log_step[l], b_re[l], b_im[l], c_re[l], c_im[l],
                         d_skip[l], w_glu[l], b_glu[l], attn_out_g[l], ssm_out_g[l], w_out[l],
                         norm2_g[l], w_ff1[l], w_ff2[l])
    return x
```

```python
import functools
import math

import jax
import jax.numpy as jnp
from jax import lax
from jax.experimental import pallas as pl
from jax.experimental.pallas import tpu as pltpu

F32 = jnp.float32
BF16 = jnp.bfloat16

HEAD_DIM = 128
N_HEADS = 8
ATTN_WIDTH = N_HEADS * HEAD_DIM
SSM_GROUP = 16
STATE_DIM = 64
CHUNK = 128
BAND = 128
DILATIONS = (1, 4, 16)
N_MOD = 6
EPS = 1e-6
NEG = -1e30

V7X_VMEM_BYTES = 64 * 1024 * 1024
VMEM_LIMIT = V7X_VMEM_BYTES - 8 * 1024 * 1024


def _params(n_axes):
    return pltpu.CompilerParams(
        dimension_semantics=("arbitrary",) * n_axes, vmem_limit_bytes=VMEM_LIMIT)


def _rms(x, eps=EPS):
    return x * lax.rsqrt(jnp.mean(x * x, axis=-1, keepdims=True) + eps)


def _adaln_kernel(c_ref, w_ref, b_ref, o_ref):
    c = c_ref[...]
    s = (c * jax.nn.sigmoid(c)).astype(BF16)
    o_ref[...] = jnp.dot(s, w_ref[...].astype(BF16), preferred_element_type=F32) + b_ref[...]


def _adaln(c, w_ada, b_ada):
    bsz, d = c.shape
    n_out = w_ada.shape[1]
    rows = 8
    tn = 1024
    c8 = jnp.zeros((rows, d), F32).at[:bsz].set(c)
    out = pl.pallas_call(
        _adaln_kernel,
        grid=(n_out // tn,),
        in_specs=[pl.BlockSpec((rows, d), lambda n: (0, 0)),
                  pl.BlockSpec((d, tn), lambda n: (0, n)),
                  pl.BlockSpec((1, tn), lambda n: (0, n))],
        out_specs=pl.BlockSpec((rows, tn), lambda n: (0, n)),
        out_shape=jax.ShapeDtypeStruct((rows, n_out), F32),
        compiler_params=_params(1),
        name="adaln",
    )(c8, w_ada, b_ada.reshape(1, n_out))
    return out[:bsz]


def _inproj_kernel(x_ref, mod_ref, g1_ref, w_ref, qg_ref, kg_ref, qkv_ref, u_ref, h_ref):
    n = pl.program_id(1)

    @pl.when(n == 0)
    def _():
        shift = mod_ref[0, 0:1, :]
        scale = mod_ref[0, 1:2, :]
        y = _rms(x_ref[...]) * g1_ref[...]
        h_ref[...] = (y * (1.0 + scale) + shift).astype(BF16)

    acc = jnp.dot(h_ref[...], w_ref[...], preferred_element_type=F32)

    def head_norm(gain):
        for hd in range(N_HEADS):
            sl = slice(hd * HEAD_DIM, (hd + 1) * HEAD_DIM)
            qkv_ref[:, sl] = (_rms(acc[:, sl]) * gain).astype(BF16)

    @pl.when(n == 0)
    def _():
        head_norm(qg_ref[...] * (HEAD_DIM ** -0.5))

    @pl.when(n == 1)
    def _():
        head_norm(kg_ref[...])

    @pl.when(n == 2)
    def _():
        qkv_ref[...] = acc.astype(BF16)

    @pl.when(n == 3)
    def _():
        u_ref[...] = acc


def _inproj(x2, mod3, norm1_g, w_in_bf, q_norm_g, k_norm_g, seq):
    t, d = x2.shape
    tm = 1024
    tn = ATTN_WIDTH
    assert w_in_bf.shape[1] == 4 * tn and seq % tm == 0
    per_batch = seq // tm
    return pl.pallas_call(
        _inproj_kernel,
        grid=(t // tm, 4),
        in_specs=[pl.BlockSpec((tm, d), lambda m, n: (m, 0)),
                  pl.BlockSpec((1, N_MOD, d), lambda m, n: (m // per_batch, 0, 0)),
                  pl.BlockSpec((1, d), lambda m, n: (0, 0)),
                  pl.BlockSpec((d, tn), lambda m, n: (0, n)),
                  pl.BlockSpec((1, HEAD_DIM), lambda m, n: (0, 0)),
                  pl.BlockSpec((1, HEAD_DIM), lambda m, n: (0, 0))],
        out_specs=[pl.BlockSpec((tm, tn), lambda m, n: (m, jnp.minimum(n, 2))),
                   pl.BlockSpec((tm, tn), lambda m, n: (m, 0))],
        out_shape=[jax.ShapeDtypeStruct((t, 3 * tn), BF16),
                   jax.ShapeDtypeStruct((t, tn), F32)],
        scratch_shapes=[pltpu.VMEM((tm, d), BF16)],
        compiler_params=_params(2),
        name="inproj",
    )(x2, mod3, norm1_g.reshape(1, d), w_in_bf, q_norm_g.reshape(1, HEAD_DIM),
      k_norm_g.reshape(1, HEAD_DIM))


ATTN_SUPER = BAND * max(DILATIONS)
ATTN_PAD = BAND * max(DILATIONS)
BLOCKS_PER_SUPER = ATTN_SUPER // BAND


def _attn_kernel(slope_ref, q_ref, k_ref, v_ref, o_ref,
                 xq, xk, xv, bias_ref, oacc, mstat, lstat, *, seq):
    n_pat = len(DILATIONS)
    copy_rows = 512

    xk[0:ATTN_PAD, :] = jnp.zeros((ATTN_PAD, HEAD_DIM), F32)
    xv[0:ATTN_PAD, :] = jnp.zeros((ATTN_PAD, HEAD_DIM), F32)

    def copy_in(c, carry):
        r = pl.multiple_of(c * copy_rows, copy_rows)
        xq[pl.ds(r, copy_rows), :] = q_ref[pl.ds(r, copy_rows), :].astype(F32)
        xk[pl.ds(ATTN_PAD + r, copy_rows), :] = k_ref[pl.ds(r, copy_rows), :].astype(F32)
        xv[pl.ds(ATTN_PAD + r, copy_rows), :] = v_ref[pl.ds(r, copy_rows), :].astype(F32)
        return carry
    lax.fori_loop(0, seq // copy_rows, copy_in, 0)

    qi = lax.broadcasted_iota(jnp.int32, (BAND, 2 * BAND), 0)
    ki = lax.broadcasted_iota(jnp.int32, (BAND, 2 * BAND), 1)
    steps = qi - ki + BAND
    valid = (steps >= 0) & (steps <= BAND)
    slope = slope_ref[...]
    for p, d in enumerate(DILATIONS):
        bias = jnp.where(valid, -(slope * float(d)) * steps.astype(F32), NEG)
        bias_ref[2 * p + 1] = bias
        bias_ref[2 * p] = jnp.where(ki >= BAND, bias, NEG)

    def rows(start, size, d):
        return pl.ds(start, size) if d == 1 else pl.ds(start, size, stride=d)

    def block(p, d, base, local):
        q = xq[rows(base, BAND, d), :].astype(BF16)
        kstart = ATTN_PAD + base - BAND * d
        k2 = xk[rows(kstart, 2 * BAND, d), :].astype(BF16)
        v2 = xv[rows(kstart, 2 * BAND, d), :].astype(BF16)
        s = lax.dot_general(q, k2, (((1,), (1,)), ((), ())), preferred_element_type=F32)
        has_prev = (base >= BAND * d).astype(jnp.int32)
        s = s + bias_ref[2 * p + has_prev]
        m = jnp.max(s, axis=-1, keepdims=True)
        e = jnp.exp(s - m)
        l = jnp.sum(e, axis=-1, keepdims=True)
        acc = jnp.dot(e.astype(BF16), v2, preferred_element_type=F32)
        oacc[p, rows(local, BAND, d), :] = acc
        mstat[p, rows(local, BAND, d), :] = jnp.broadcast_to(m, (BAND, HEAD_DIM))
        lstat[p, rows(local, BAND, d), :] = jnp.broadcast_to(l, (BAND, HEAD_DIM))

    merge_rows = 256

    def super_block(sb, carry):
        p0 = sb * ATTN_SUPER

        def blocks(idx, c2):
            for p, d in enumerate(DILATIONS):
                r = idx & (d - 1)
                j = idx >> int(math.log2(d))
                local = r + BAND * d * j
                block(p, d, p0 + local, local)
            return c2
        lax.fori_loop(0, BLOCKS_PER_SUPER, blocks, 0)

        def merge(c, c2):
            r = pl.multiple_of(c * merge_rows, merge_rows)
            ms = [mstat[p, pl.ds(r, merge_rows), :] for p in range(n_pat)]
            mx = functools.reduce(jnp.maximum, ms)
            num = jnp.zeros((merge_rows, HEAD_DIM), F32)
            den = jnp.zeros((merge_rows, HEAD_DIM), F32)
            for p in range(n_pat):
                w = jnp.exp(ms[p] - mx)
                num = num + w * oacc[p, pl.ds(r, merge_rows), :]
                den = den + w * lstat[p, pl.ds(r, merge_rows), :]
            o_ref[pl.ds(pl.multiple_of(p0 + r, merge_rows), merge_rows), :] = (num / den).astype(BF16)
            return c2
        lax.fori_loop(0, ATTN_SUPER // merge_rows, merge, 0)
        return carry
    lax.fori_loop(0, seq // ATTN_SUPER, super_block, 0)


def _attention(qkv3, slopes):
    bsz, seq, _ = qkv3.shape
    assert seq % ATTN_SUPER == 0
    blk = lambda off: pl.BlockSpec((None, seq, HEAD_DIM), lambda b, h: (b, 0, off + h))
    return pl.pallas_call(
        functools.partial(_attn_kernel, seq=seq),
        grid=(bsz, N_HEADS),
        in_specs=[pl.BlockSpec((None, 1, 2 * BAND), lambda b, h: (h, 0, 0)),
                  blk(0), blk(N_HEADS), blk(2 * N_HEADS)],
        out_specs=pl.BlockSpec((None, seq, HEAD_DIM), lambda b, h: (b, 0, h)),
        out_shape=jax.ShapeDtypeStruct((bsz, seq, ATTN_WIDTH), BF16),
        scratch_shapes=[pltpu.VMEM((seq, HEAD_DIM), F32),
                        pltpu.VMEM((ATTN_PAD + seq, HEAD_DIM), F32),
                        pltpu.VMEM((ATTN_PAD + seq, HEAD_DIM), F32),
                        pltpu.VMEM((2 * len(DILATIONS), BAND, 2 * BAND), F32),
                        pltpu.VMEM((len(DILATIONS), ATTN_SUPER, HEAD_DIM), F32),
                        pltpu.VMEM((len(DILATIONS), ATTN_SUPER, HEAD_DIM), F32),
                        pltpu.VMEM((len(DILATIONS), ATTN_SUPER, HEAD_DIM), F32)],
        compiler_params=_params(2),
        name="attention",
    )(slopes, qkv3, qkv3, qkv3)


N_DOUBLINGS = 8


def _cmul(ar, ai, br, bi):
    return ar * br - ai * bi, ar * bi + ai * br


def _cpow(ar, ai, e, nbits):
    shape = jnp.broadcast_shapes(ar.shape, e.shape)
    pr = jnp.ones(shape, F32)
    pi = jnp.zeros(shape, F32)
    br, bi = ar, ai
    for k in range(nbits):
        nr, ni = _cmul(pr, pi, br, bi)
        sel = ((e >> k) & 1) == 1
        pr = jnp.where(sel, nr, pr)
        pi = jnp.where(sel, ni, pi)
        br, bi = _cmul(br, bi, br, bi)
    return pr, pi


def _zoh(lr, li, log_step):
    dt = jnp.exp(log_step)
    mag = jnp.exp(lr * dt)
    ar = mag * jnp.cos(li * dt)
    ai = mag * jnp.sin(li * dt)
    den = lr * lr + li * li
    cr = ((ar - 1.0) * lr + ai * li) / den
    ci = (ai * lr - (ar - 1.0) * li) / den
    return ar, ai, cr, ci


def _s5_params_kernel(lr_row, li_row, ls_row, lr_col, li_col, ls_col,
                      bt_re, bt_im, c_re, c_im, ct_re, ct_im,
                      krow_ref, w_ref, v_ref, ap1_ref, ap2_ref):
    p_dim = STATE_DIM
    ar, ai, cr, ci = _zoh(lr_row[...], li_row[...], ls_row[...])
    bbr, bbi = _cmul(cr, ci, bt_re[...], bt_im[...])
    s_idx = lax.broadcasted_iota(jnp.int32, (CHUNK, p_dim), 0)
    qr, qi = _cpow(ar, ai, CHUNK - 1 - s_idx, 7)
    for j in range(SSM_GROUP):
        wr, wi = _cmul(qr, qi, bbr[j:j + 1, :], bbi[j:j + 1, :])
        w_ref[j * CHUNK:(j + 1) * CHUNK, :] = jnp.concatenate([wr, wi], axis=-1).astype(BF16)
    mr, mi = ar, ai
    for _ in range(7):
        mr, mi = _cmul(mr, mi, mr, mi)
    for k in range(N_DOUBLINGS):
        ap1_ref[k:k + 1, :] = jnp.concatenate([mr, mr], axis=-1)
        ap2_ref[k:k + 1, :] = jnp.concatenate([-mi, mi], axis=-1)
        mr, mi = _cmul(mr, mi, mr, mi)

    acr, aci, _, _ = _zoh(lr_col[...], li_col[...], ls_col[...])
    t_idx = lax.broadcasted_iota(jnp.int32, (p_dim, CHUNK), 1)
    pr, pi = _cpow(acr, aci, t_idx, 7)
    cbr, cbi = [], []
    for j in range(SSM_GROUP):
        r_, i_ = _cmul(c_re[...], c_im[...], bbr[j:j + 1, :], bbi[j:j + 1, :])
        cbr.append(r_)
        cbi.append(i_)
    cbr = jnp.concatenate(cbr, axis=0)
    cbi = jnp.concatenate(cbi, axis=0)
    hi = lax.Precision.HIGHEST
    krow_ref[...] = (jnp.dot(cbr, pr, precision=hi, preferred_element_type=F32)
                     - jnp.dot(cbi, pi, precision=hi, preferred_element_type=F32))
    p1r, p1i = _cmul(pr, pi, acr, aci)
    ctr = ct_re[...]
    cti = ct_im[...]
    for i in range(SSM_GROUP):
        vr, vi = _cmul(ctr[:, i:i + 1], cti[:, i:i + 1], p1r, p1i)
        v_ref[0:p_dim, i * CHUNK:(i + 1) * CHUNK] = vr.astype(BF16)
        v_ref[p_dim:2 * p_dim, i * CHUNK:(i + 1) * CHUNK] = (-vi).astype(BF16)


def _s5_params(lam_re, lam_im, log_step, b_re, b_im, c_re, c_im):
    g, p = lam_re.shape
    n = SSM_GROUP
    row = lambda a: a.reshape(g, 1, p)
    col = lambda a: a.reshape(g, p, 1)
    ls = jnp.broadcast_to(log_step[:, None], (g, p))
    tr = lambda a: jnp.swapaxes(a, 1, 2)
    spec = lambda *shape: pl.BlockSpec((None,) + shape, lambda i: (i,) + (0,) * len(shape))
    return pl.pallas_call(
        _s5_params_kernel,
        grid=(g,),
        in_specs=[spec(1, p)] * 3 + [spec(p, 1)] * 3 + [spec(n, p)] * 4 + [spec(p, n)] * 2,
        out_specs=[spec(n * n, CHUNK), spec(n * CHUNK, 2 * p), spec(2 * p, n * CHUNK),
                   spec(N_DOUBLINGS, 2 * p), spec(N_DOUBLINGS, 2 * p)],
        out_shape=[jax.ShapeDtypeStruct((g, n * n, CHUNK), F32),
                   jax.ShapeDtypeStruct((g, n * CHUNK, 2 * p), BF16),
                   jax.ShapeDtypeStruct((g, 2 * p, n * CHUNK), BF16),
                   jax.ShapeDtypeStruct((g, N_DOUBLINGS, 2 * p), F32),
                   jax.ShapeDtypeStruct((g, N_DOUBLINGS, 2 * p), F32)],
        compiler_params=_params(1),
        name="s5_params",
    )(row(lam_re), row(lam_im), row(ls), col(lam_re), col(lam_im), col(ls),
      tr(b_re), tr(b_im), c_re, c_im, tr(c_re), tr(c_im))


def _s5_kernel(u_ref, krow_ref, w_ref, v_ref, ap1_ref, ap2_ref, d_ref, y_ref, t_ref,
               *, chunks_per_seq):
    n = SSM_GROUP
    width = n * CHUNK
    n_chunks = u_ref.shape[0]

    s_idx = lax.broadcasted_iota(jnp.int32, (CHUNK, CHUNK), 0)
    t_idx = lax.broadcasted_iota(jnp.int32, (CHUNK, CHUNK), 1)
    causal = t_idx >= s_idx

    def build(j, carry):
        for i in range(n):
            taps = jnp.broadcast_to(krow_ref[pl.ds(j * n + i, 1), :], (CHUNK, CHUNK))
            shifted = pltpu.roll(taps, 0, 1, stride=1, stride_axis=0)
            t_ref[pl.ds(pl.multiple_of(j * CHUNK, CHUNK), CHUNK), i * CHUNK:(i + 1) * CHUNK] = (
                jnp.where(causal, shifted, 0.0).astype(BF16))
        return carry
    lax.fori_loop(0, n, build, 0)

    u = u_ref[...]
    ub = u.astype(BF16)

    b = jnp.dot(ub, w_ref[...], preferred_element_type=F32)
    pos = lax.broadcasted_iota(jnp.int32, (n_chunks, 2 * STATE_DIM), 0) & (chunks_per_seq - 1)

    def shift_rows(x, k):
        return jnp.where(pos >= k, pltpu.roll(x, k, 0), 0.0)

    h = shift_rows(b, 1)
    for k in range(int(math.log2(chunks_per_seq))):
        hs = shift_rows(h, 1 << k)
        h = h + hs * ap1_ref[k:k + 1, :] + pltpu.roll(hs, STATE_DIM, 1) * ap2_ref[k:k + 1, :]

    y = jnp.dot(ub, t_ref[...], preferred_element_type=F32)
    y = y + jnp.dot(h.astype(BF16), v_ref[...], preferred_element_type=F32)
    y = y + d_ref[...] * u
    y_ref[...] = jax.nn.gelu(y)


def _s5_main(u_g, krow, w, v, ap1, ap2, d_rep, chunks_per_seq):
    g, n_chunks, width = u_g.shape
    assert chunks_per_seq & (chunks_per_seq - 1) == 0 and chunks_per_seq <= 1 << N_DOUBLINGS
    spec = lambda *shape: pl.BlockSpec((None,) + shape, lambda i: (i,) + (0,) * len(shape))
    return pl.pallas_call(
        functools.partial(_s5_kernel, chunks_per_seq=chunks_per_seq),
        grid=(g,),
        in_specs=[spec(n_chunks, width), spec(SSM_GROUP * SSM_GROUP, CHUNK),
                  spec(width, 2 * STATE_DIM), spec(2 * STATE_DIM, width),
                  spec(N_DOUBLINGS, 2 * STATE_DIM), spec(N_DOUBLINGS, 2 * STATE_DIM),
                  spec(1, width)],
        out_specs=spec(n_chunks, width),
        out_shape=jax.ShapeDtypeStruct((g, n_chunks, width), F32),
        scratch_shapes=[pltpu.VMEM((width, width), BF16)],
        compiler_params=_params(1),
        name="s5_main",
    )(u_g, krow, w, v, ap1, ap2, d_rep)


def _glu_kernel(y_ref, w_ref, b_ref, g_ref, o_ref):
    y = y_ref[...]
    z = jnp.dot(y.astype(BF16), w_ref[...], preferred_element_type=F32) + b_ref[...]
    ssm = y * jax.nn.sigmoid(z)
    o_ref[...] = (_rms(ssm) * g_ref[...]).astype(BF16)


def _glu(y2, w_glu_bf, b_glu, ssm_out_g):
    t, w = y2.shape
    tm = 1024
    return pl.pallas_call(
        _glu_kernel,
        grid=(t // tm,),
        in_specs=[pl.BlockSpec((tm, w), lambda m: (m, 0)),
                  pl.BlockSpec((w, w), lambda m: (0, 0)),
                  pl.BlockSpec((1, w), lambda m: (0, 0)),
                  pl.BlockSpec((1, w), lambda m: (0, 0))],
        out_specs=pl.BlockSpec((tm, w), lambda m: (m, 0)),
        out_shape=jax.ShapeDtypeStruct((t, w), BF16),
        compiler_params=_params(1),
        name="glu",
    )(y2, w_glu_bf, b_glu.reshape(1, w), ssm_out_g.reshape(1, w))


def _outproj_kernel(attn_ref, ssm_ref, x_ref, mod_ref, ag_ref, wa_ref, ws_ref, g2_ref,
                    x1_ref, h2_ref):
    attn_n = (_rms(attn_ref[...].astype(F32)) * ag_ref[...]).astype(BF16)
    mixed = jnp.dot(attn_n, wa_ref[...], preferred_element_type=F32)
    mixed = mixed + jnp.dot(ssm_ref[...], ws_ref[...], preferred_element_type=F32)
    gate1 = mod_ref[0, 2:3, :]
    shift2 = mod_ref[0, 3:4, :]
    scale2 = mod_ref[0, 4:5, :]
    x1 = x_ref[...] + gate1 * mixed
    x1_ref[...] = x1
    h2_ref[...] = (_rms(x1) * g2_ref[...] * (1.0 + scale2) + shift2).astype(BF16)


def _outproj(attn2, ssm2, x2, mod3, attn_out_g, w_out_bf, norm2_g, seq):
    t, d = x2.shape
    wa = attn2.shape[1]
    ws = ssm2.shape[1]
    tm = 512
    per_batch = seq // tm
    return pl.pallas_call(
        _outproj_kernel,
        grid=(t // tm,),
        in_specs=[pl.BlockSpec((tm, wa), lambda m: (m, 0)),
                  pl.BlockSpec((tm, ws), lambda m: (m, 0)),
                  pl.BlockSpec((tm, d), lambda m: (m, 0)),
                  pl.BlockSpec((1, N_MOD, d), lambda m: (m // per_batch, 0, 0)),
                  pl.BlockSpec((1, wa), lambda m: (0, 0)),
                  pl.BlockSpec((wa, d), lambda m: (0, 0)),
                  pl.BlockSpec((ws, d), lambda m: (1, 0)),
                  pl.BlockSpec((1, d), lambda m: (0, 0))],
        out_specs=[pl.BlockSpec((tm, d), lambda m: (m, 0)),
                   pl.BlockSpec((tm, d), lambda m: (m, 0))],
        out_shape=[jax.ShapeDtypeStruct((t, d), F32),
                   jax.ShapeDtypeStruct((t, d), BF16)],
        compiler_params=_params(1),
        name="outproj",
    )(attn2, ssm2, x2, mod3, attn_out_g.reshape(1, wa), w_out_bf, w_out_bf,
      norm2_g.reshape(1, d))


def _ffn_kernel(h_ref, w1_ref, w2_ref, x1_ref, mod_ref, o_ref):
    f = pl.program_id(1)
    a = jnp.dot(h_ref[...], w1_ref[...], preferred_element_type=F32)
    a = jnp.square(jnp.maximum(a, 0.0)).astype(BF16)
    part = jnp.dot(a, w2_ref[...], preferred_element_type=F32)

    @pl.when(f == 0)
    def _():
        o_ref[...] = part

    @pl.when(f > 0)
    def _():
        o_ref[...] += part

    @pl.when(f == pl.num_programs(1) - 1)
    def _():
        o_ref[...] = x1_ref[...] + mod_ref[0, 5:6, :] * o_ref[...]


def _ffn(h2, w1_bf, w2_bf, x1, mod3, seq):
    t, d = x1.shape
    dff = w1_bf.shape[1]
    tm = 512
    tf = 1024
    per_batch = seq // tm
    return pl.pallas_call(
        _ffn_kernel,
        grid=(t // tm, dff // tf),
        in_specs=[pl.BlockSpec((tm, d), lambda m, f: (m, 0)),
                  pl.BlockSpec((d, tf), lambda m, f: (0, f)),
                  pl.BlockSpec((tf, d), lambda m, f: (f, 0)),
                  pl.BlockSpec((tm, d), lambda m, f: (m, 0)),
                  pl.BlockSpec((1, N_MOD, d), lambda m, f: (m // per_batch, 0, 0))],
        out_specs=pl.BlockSpec((tm, d), lambda m, f: (m, 0)),
        out_shape=jax.ShapeDtypeStruct((t, d), F32),
        compiler_params=_params(2),
        name="ffn",
    )(h2, w1_bf, w2_bf, x1, mod3)


def _layer(x, c, w_ada, b_ada, norm1_g, w_in, q_norm_g, k_norm_g, lam_re, lam_im, log_step,
           b_re, b_im, c_re, c_im, d_skip, w_glu, b_glu, attn_out_g, ssm_out_g, w_out,
           norm2_g, w_ff1, w_ff2):
    bsz, seq, d = x.shape
    t = bsz * seq
    n_groups = lam_re.shape[0]
    ssm_width = n_groups * SSM_GROUP
    chunks_per_seq = seq // CHUNK

    mod3 = _adaln(c, w_ada, b_ada).reshape(bsz, N_MOD, d)
    x2 = x.reshape(t, d)
    qkv, u = _inproj(x2, mod3, norm1_g, w_in.astype(BF16), q_norm_g, k_norm_g, seq)

    slopes = 2.0 ** (-8.0 * (jnp.arange(N_HEADS, dtype=F32) + 1.0) / N_HEADS)
    slopes = jnp.broadcast_to(slopes[:, None, None], (N_HEADS, 1, 2 * BAND))
    attn = _attention(qkv.reshape(bsz, seq, 3 * ATTN_WIDTH), slopes)

    krow, w_s, v_s, ap1, ap2 = _s5_params(lam_re, lam_im, log_step, b_re, b_im, c_re, c_im)
    u_g = u.reshape(bsz * chunks_per_seq, CHUNK, n_groups, SSM_GROUP).transpose(2, 0, 3, 1)
    u_g = u_g.reshape(n_groups, bsz * chunks_per_seq, SSM_GROUP * CHUNK)
    d_rep = jnp.repeat(d_skip.reshape(n_groups, SSM_GROUP), CHUNK, axis=1)
    y_g = _s5_main(u_g, krow, w_s, v_s, ap1, ap2, d_rep.reshape(n_groups, 1, SSM_GROUP * CHUNK),
                   chunks_per_seq)
    y2 = y_g.reshape(n_groups, bsz * chunks_per_seq, SSM_GROUP, CHUNK).transpose(1, 3, 0, 2)
    y2 = y2.reshape(t, ssm_width)

    ssm = _glu(y2, w_glu.astype(BF16), b_glu, ssm_out_g)
    x1, h2 = _outproj(attn.reshape(t, ATTN_WIDTH), ssm, x2, mod3, attn_out_g,
                      w_out.astype(BF16), norm2_g, seq)
    out = _ffn(h2, w_ff1.astype(BF16), w_ff2.astype(BF16), x1, mod3, seq)
    return out.reshape(bsz, seq, d)


def kernel(x, c, w_ada, b_ada, norm1_g, w_in, q_norm_g, k_norm_g, lam_re, lam_im, log_step,
           b_re, b_im, c_re, c_im, d_skip, w_glu, b_glu, attn_out_g, ssm_out_g, w_out,
           norm2_g, w_ff1, w_ff2):
    for l in range(w_ada.shape[0]):
        x = _layer(x, c, w_ada[l], b_ada[l], norm1_g[l], w_in[l], q_norm_g[l], k_norm_g[l],
                   lam_re[l], lam_im[l], log_step[l], b_re[l], b_im[l], c_re[l], c_im[l],
                   d_skip[l], w_glu[l], b_glu[l], attn_out_g[l], ssm_out_g[l], w_out[l],
                   norm2_g[l], w_ff1[l], w_ff2[l])
    return x
```

```python
import functools
import math

import jax
import jax.numpy as jnp
from jax import lax
from jax.experimental import pallas as pl
from jax.experimental.pallas import tpu as pltpu

F32 = jnp.float32
BF16 = jnp.bfloat16

HEAD_DIM = 128
N_HEADS = 8
ATTN_WIDTH = N_HEADS * HEAD_DIM
SSM_GROUP = 16
STATE_DIM = 64
CHUNK = 128
BAND = 128
DILATIONS = (1, 4, 16)
N_MOD = 6
EPS = 1e-6
NEG = -1e30
LOG2E = math.log2(math.e)

V7X_VMEM_BYTES = 64 * 1024 * 1024
VMEM_LIMIT = V7X_VMEM_BYTES - 8 * 1024 * 1024


def _params(n_axes):
    return pltpu.CompilerParams(
        dimension_semantics=("arbitrary",) * n_axes, vmem_limit_bytes=VMEM_LIMIT)


def _rms(x, eps=EPS):
    return x * lax.rsqrt(jnp.mean(x * x, axis=-1, keepdims=True) + eps)


def _adaln_kernel(c_ref, w_ref, b_ref, o_ref):
    c = c_ref[...]
    s = (c * jax.nn.sigmoid(c)).astype(BF16)
    o_ref[...] = jnp.dot(s, w_ref[...].astype(BF16), preferred_element_type=F32) + b_ref[...]


def _adaln(c, w_ada, b_ada):
    bsz, d = c.shape
    n_out = w_ada.shape[1]
    rows = 8
    tn = 1024
    c8 = jnp.zeros((rows, d), F32).at[:bsz].set(c)
    out = pl.pallas_call(
        _adaln_kernel,
        grid=(n_out // tn,),
        in_specs=[pl.BlockSpec((rows, d), lambda n: (0, 0)),
                  pl.BlockSpec((d, tn), lambda n: (0, n)),
                  pl.BlockSpec((1, tn), lambda n: (0, n))],
        out_specs=pl.BlockSpec((rows, tn), lambda n: (0, n)),
        out_shape=jax.ShapeDtypeStruct((rows, n_out), F32),
        compiler_params=_params(1),
        name="adaln",
    )(c8, w_ada, b_ada.reshape(1, n_out))
    return out[:bsz]


def _inproj_kernel(x_ref, mod_ref, g1_ref, w_ref, qg_ref, kg_ref, qkv_ref, u_ref, h_ref):
    n = pl.program_id(1)

    @pl.when(n == 0)
    def _():
        shift = mod_ref[0, 0:1, :]
        scale = mod_ref[0, 1:2, :]
        y = _rms(x_ref[...]) * g1_ref[...]
        h_ref[...] = (y * (1.0 + scale) + shift).astype(BF16)

    def project():
        return jnp.dot(h_ref[...], w_ref[...], preferred_element_type=F32)

    def head_norm(gain):
        for pair in range(N_HEADS // 2):
            c0 = pair * 2 * HEAD_DIM
            acc = jnp.dot(h_ref[...], w_ref[:, c0:c0 + 2 * HEAD_DIM], preferred_element_type=F32)
            for hd in range(2):
                sl = slice(hd * HEAD_DIM, (hd + 1) * HEAD_DIM)
                qkv_ref[:, c0 + hd * HEAD_DIM:c0 + (hd + 1) * HEAD_DIM] = (
                    _rms(acc[:, sl]) * gain).astype(BF16)

    @pl.when(n == 0)
    def _():
        head_norm(qg_ref[...] * (HEAD_DIM ** -0.5 * LOG2E))

    @pl.when(n == 1)
    def _():
        head_norm(kg_ref[...])

    @pl.when(n == 2)
    def _():
        qkv_ref[...] = project().astype(BF16)

    @pl.when(n == 3)
    def _():
        u_ref[...] = project()


def _inproj(x2, mod3, norm1_g, w_in_bf, q_norm_g, k_norm_g, seq):
    t, d = x2.shape
    tm = 1024
    tn = ATTN_WIDTH
    assert w_in_bf.shape[1] == 4 * tn and seq % tm == 0
    per_batch = seq // tm
    return pl.pallas_call(
        _inproj_kernel,
        grid=(t // tm, 4),
        in_specs=[pl.BlockSpec((tm, d), lambda m, n: (m, 0)),
                  pl.BlockSpec((1, N_MOD, d), lambda m, n: (m // per_batch, 0, 0)),
                  pl.BlockSpec((1, d), lambda m, n: (0, 0)),
                  pl.BlockSpec((d, tn), lambda m, n: (0, n)),
                  pl.BlockSpec((1, HEAD_DIM), lambda m, n: (0, 0)),
                  pl.BlockSpec((1, HEAD_DIM), lambda m, n: (0, 0))],
        out_specs=[pl.BlockSpec((tm, tn), lambda m, n: (m, jnp.minimum(n, 2))),
                   pl.BlockSpec((tm, tn), lambda m, n: (m, 0))],
        out_shape=[jax.ShapeDtypeStruct((t, 3 * tn), BF16),
                   jax.ShapeDtypeStruct((t, tn), F32)],
        scratch_shapes=[pltpu.VMEM((tm, d), BF16)],
        compiler_params=_params(2),
        name="inproj",
    )(x2, mod3, norm1_g.reshape(1, d), w_in_bf, q_norm_g.reshape(1, HEAD_DIM),
      k_norm_g.reshape(1, HEAD_DIM))


ATTN_SUPER = BAND * max(DILATIONS)
BLOCKS_PER_SUPER = ATTN_SUPER // BAND


def _attn_kernel(slope_ref, q_ref, k_ref, v_ref, o_ref,
                 stage, stage4, q4, q16, k1, k4, k16, v1, v4, v16,
                 bias_ref, oacc, mstat, lstat, *, seq):
    assert DILATIONS == (1, 4, 16)
    n_pat = len(DILATIONS)
    n4 = ATTN_SUPER // 4
    n16 = ATTN_SUPER // 16

    def deinterleave(src, a1, a4, a16, pad):
        def chunk(c, carry):
            r0 = pl.multiple_of(c * ATTN_SUPER, ATTN_SUPER)
            x = src[pl.ds(r0, ATTN_SUPER), :]
            if a1 is not None:
                a1[0, pl.ds(pad + r0, ATTN_SUPER), :] = x
            stage[...] = x.astype(F32)
            for r in range(4):
                y = stage[pl.ds(r, n4, stride=4), :]
                stage4[r * n4:(r + 1) * n4, :] = y
                a4[r, pl.ds(pad + pl.multiple_of(c * n4, n4), n4), :] = y.astype(BF16)
            for r in range(4):
                for a in range(4):
                    z = stage4[pl.ds(r * n4 + a, n16, stride=4), :]
                    a16[4 * a + r, pl.ds(pad + pl.multiple_of(c * n16, n16), n16), :] = (
                        z.astype(BF16))
            return carry
        lax.fori_loop(0, seq // ATTN_SUPER, chunk, 0)

    for buf in (k1, k4, k16, v1, v4, v16):
        buf[:, 0:BAND, :] = jnp.zeros((buf.shape[0], BAND, HEAD_DIM), BF16)
    deinterleave(q_ref, None, q4, q16, 0)
    deinterleave(k_ref, k1, k4, k16, BAND)
    deinterleave(v_ref, v1, v4, v16, BAND)

    qi = lax.broadcasted_iota(jnp.int32, (BAND, 2 * BAND), 0)
    ki = lax.broadcasted_iota(jnp.int32, (BAND, 2 * BAND), 1)
    steps = qi - ki + BAND
    valid = (steps >= 0) & (steps <= BAND)
    slope = slope_ref[...]
    for p, d in enumerate(DILATIONS):
        bias = jnp.where(valid, -(slope * (float(d) * LOG2E)) * steps.astype(F32), NEG)
        bias_ref[2 * p + 1] = bias
        bias_ref[2 * p] = jnp.where(ki >= BAND, bias, NEG)

    q_streams = (None, q4, q16)
    k_streams = (k1, k4, k16)
    v_streams = (v1, v4, v16)
    ones = jnp.ones((2 * BAND, HEAD_DIM), BF16)

    def rows(start, size, d):
        return pl.ds(start, size) if d == 1 else pl.ds(start, size, stride=d)

    def block(p, d, r, j, local):
        row0 = pl.multiple_of(j * BAND, BAND)
        if d == 1:
            q = q_ref[pl.ds(row0, BAND), :]
        else:
            q = q_streams[p][r, pl.ds(row0, BAND), :]
        k2 = k_streams[p][r, pl.ds(row0, 2 * BAND), :]
        v2 = v_streams[p][r, pl.ds(row0, 2 * BAND), :]
        s = lax.dot_general(q, k2, (((1,), (1,)), ((), ())), preferred_element_type=F32)
        has_prev = (j >= 1).astype(jnp.int32)
        s = s + bias_ref[2 * p + has_prev]
        m = jnp.max(s, axis=-1, keepdims=True)
        e = jnp.exp2(s - m).astype(BF16)
        acc = jnp.dot(e, jnp.concatenate([v2, ones], axis=1), preferred_element_type=F32)
        oacc[p, rows(local, BAND, d), :] = acc[:, :HEAD_DIM]
        lstat[p, rows(local, BAND, d), :] = acc[:, HEAD_DIM:]
        mstat[p, rows(local, BAND, d), :] = jnp.broadcast_to(m, (BAND, HEAD_DIM))

    merge_rows = 256

    def super_block(sb, carry):
        p0 = sb * ATTN_SUPER

        def blocks(idx, c2):
            for p, d in enumerate(DILATIONS):
                r = idx & (d - 1)
                jl = idx >> int(math.log2(d))
                block(p, d, r, sb * (BLOCKS_PER_SUPER // d) + jl, r + BAND * d * jl)
            return c2
        lax.fori_loop(0, BLOCKS_PER_SUPER, blocks, 0, unroll=4)

        def merge(c, c2):
            r = pl.multiple_of(c * merge_rows, merge_rows)
            ms = [mstat[p, pl.ds(r, merge_rows), :] for p in range(n_pat)]
            mx = functools.reduce(jnp.maximum, ms)
            num = jnp.zeros((merge_rows, HEAD_DIM), F32)
            den = jnp.zeros((merge_rows, HEAD_DIM), F32)
            for p in range(n_pat):
                w = jnp.exp2(ms[p] - mx)
                num = num + w * oacc[p, pl.ds(r, merge_rows), :]
                den = den + w * lstat[p, pl.ds(r, merge_rows), :]
            o_ref[pl.ds(pl.multiple_of(p0 + r, merge_rows), merge_rows), :] = (num / den).astype(BF16)
            return c2
        lax.fori_loop(0, ATTN_SUPER // merge_rows, merge, 0)
        return carry
    lax.fori_loop(0, seq // ATTN_SUPER, super_block, 0)


def _attention(qkv3, slopes):
    bsz, seq, _ = qkv3.shape
    assert seq % ATTN_SUPER == 0
    blk = lambda off: pl.BlockSpec((None, seq, HEAD_DIM), lambda b, h: (b, 0, off + h))
    stream = lambda d, pad: pltpu.VMEM((d, pad + seq // d, HEAD_DIM), BF16)
    stat = pltpu.VMEM((len(DILATIONS), ATTN_SUPER, HEAD_DIM), F32)
    return pl.pallas_call(
        functools.partial(_attn_kernel, seq=seq),
        grid=(bsz, N_HEADS),
        in_specs=[pl.BlockSpec((None, 1, 2 * BAND), lambda b, h: (h, 0, 0)),
                  blk(0), blk(N_HEADS), blk(2 * N_HEADS)],
        out_specs=pl.BlockSpec((None, seq, HEAD_DIM), lambda b, h: (b, 0, h)),
        out_shape=jax.ShapeDtypeStruct((bsz, seq, ATTN_WIDTH), BF16),
        scratch_shapes=[pltpu.VMEM((ATTN_SUPER, HEAD_DIM), F32),
                        pltpu.VMEM((ATTN_SUPER, HEAD_DIM), F32),
                        stream(4, 0), stream(16, 0),
                        stream(1, BAND), stream(4, BAND), stream(16, BAND),
                        stream(1, BAND), stream(4, BAND), stream(16, BAND),
                        pltpu.VMEM((2 * len(DILATIONS), BAND, 2 * BAND), F32),
                        stat, stat, stat],
        compiler_params=_params(2),
        name="attention",
    )(slopes, qkv3, qkv3, qkv3)


N_DOUBLINGS = 8


def _cmul(ar, ai, br, bi):
    return ar * br - ai * bi, ar * bi + ai * br


def _cpow(ar, ai, e, nbits):
    shape = jnp.broadcast_shapes(ar.shape, e.shape)
    pr = jnp.ones(shape, F32)
    pi = jnp.zeros(shape, F32)
    br, bi = ar, ai
    for k in range(nbits):
        nr, ni = _cmul(pr, pi, br, bi)
        sel = ((e >> k) & 1) == 1
        pr = jnp.where(sel, nr, pr)
        pi = jnp.where(sel, ni, pi)
        br, bi = _cmul(br, bi, br, bi)
    return pr, pi


def _zoh(lr, li, log_step):
    dt = jnp.exp(log_step)
    mag = jnp.exp(lr * dt)
    ar = mag * jnp.cos(li * dt)
    ai = mag * jnp.sin(li * dt)
    den = lr * lr + li * li
    cr = ((ar - 1.0) * lr + ai * li) / den
    ci = (ai * lr - (ar - 1.0) * li) / den
    return ar, ai, cr, ci


def _s5_params_kernel(lr_row, li_row, ls_row, lr_col, li_col, ls_col,
                      bt_re, bt_im, c_re, c_im, ct_re, ct_im,
                      krow_ref, w_ref, v_ref, ap1_ref, ap2_ref):
    p_dim = STATE_DIM
    ar, ai, cr, ci = _zoh(lr_row[...], li_row[...], ls_row[...])
    bbr, bbi = _cmul(cr, ci, bt_re[...], bt_im[...])
    s_idx = lax.broadcasted_iota(jnp.int32, (CHUNK, p_dim), 0)
    qr, qi = _cpow(ar, ai, CHUNK - 1 - s_idx, 7)
    for j in range(SSM_GROUP):
        wr, wi = _cmul(qr, qi, bbr[j:j + 1, :], bbi[j:j + 1, :])
        w_ref[j * CHUNK:(j + 1) * CHUNK, :] = jnp.concatenate([wr, wi], axis=-1).astype(BF16)
    mr, mi = ar, ai
    for _ in range(7):
        mr, mi = _cmul(mr, mi, mr, mi)
    for k in range(N_DOUBLINGS):
        ap1_ref[k:k + 1, :] = jnp.concatenate([mr, mr], axis=-1)
        ap2_ref[k:k + 1, :] = jnp.concatenate([-mi, mi], axis=-1)
        mr, mi = _cmul(mr, mi, mr, mi)

    acr, aci, _, _ = _zoh(lr_col[...], li_col[...], ls_col[...])
    t_idx = lax.broadcasted_iota(jnp.int32, (p_dim, CHUNK), 1)
    pr, pi = _cpow(acr, aci, t_idx, 7)
    cbr, cbi = [], []
    for j in range(SSM_GROUP):
        r_, i_ = _cmul(c_re[...], c_im[...], bbr[j:j + 1, :], bbi[j:j + 1, :])
        cbr.append(r_)
        cbi.append(i_)
    cbr = jnp.concatenate(cbr, axis=0)
    cbi = jnp.concatenate(cbi, axis=0)
    hi = lax.Precision.HIGHEST
    krow_ref[...] = (jnp.dot(cbr, pr, precision=hi, preferred_element_type=F32)
                     - jnp.dot(cbi, pi, precision=hi, preferred_element_type=F32))
    p1r, p1i = _cmul(pr, pi, acr, aci)
    ctr = ct_re[...]
    cti = ct_im[...]
    for i in range(SSM_GROUP):
        vr, vi = _cmul(ctr[:, i:i + 1], cti[:, i:i + 1], p1r, p1i)
        v_ref[0:p_dim, i * CHUNK:(i + 1) * CHUNK] = vr.astype(BF16)
        v_ref[p_dim:2 * p_dim, i * CHUNK:(i + 1) * CHUNK] = (-vi).astype(BF16)


def _s5_params(lam_re, lam_im, log_step, b_re, b_im, c_re, c_im):
    g, p = lam_re.shape
    n = SSM_GROUP
    row = lambda a: a.reshape(g, 1, p)
    col = lambda a: a.reshape(g, p, 1)
    ls = jnp.broadcast_to(log_step[:, None], (g, p))
    tr = lambda a: jnp.swapaxes(a, 1, 2)
    spec = lambda *shape: pl.BlockSpec((None,) + shape, lambda i: (i,) + (0,) * len(shape))
    return pl.pallas_call(
        _s5_params_kernel,
        grid=(g,),
        in_specs=[spec(1, p)] * 3 + [spec(p, 1)] * 3 + [spec(n, p)] * 4 + [spec(p, n)] * 2,
        out_specs=[spec(n * n, CHUNK), spec(n * CHUNK, 2 * p), spec(2 * p, n * CHUNK),
                   spec(N_DOUBLINGS, 2 * p), spec(N_DOUBLINGS, 2 * p)],
        out_shape=[jax.ShapeDtypeStruct((g, n * n, CHUNK), F32),
                   jax.ShapeDtypeStruct((g, n * CHUNK, 2 * p), BF16),
                   jax.ShapeDtypeStruct((g, 2 * p, n * CHUNK), BF16),
                   jax.ShapeDtypeStruct((g, N_DOUBLINGS, 2 * p), F32),
                   jax.ShapeDtypeStruct((g, N_DOUBLINGS, 2 * p), F32)],
        compiler_params=_params(1),
        name="s5_params",
    )(row(lam_re), row(lam_im), row(ls), col(lam_re), col(lam_im), col(ls),
      tr(b_re), tr(b_im), c_re, c_im, tr(c_re), tr(c_im))


def _s5_kernel(u_ref, krow_ref, w_ref, v_ref, ap1_ref, ap2_ref, d_ref, y_ref, t_ref,
               *, chunks_per_seq):
    n = SSM_GROUP
    width = n * CHUNK
    n_chunks = u_ref.shape[0]

    s_idx = lax.broadcasted_iota(jnp.int32, (CHUNK, CHUNK), 0)
    t_idx = lax.broadcasted_iota(jnp.int32, (CHUNK, CHUNK), 1)
    causal = t_idx >= s_idx

    def build(j, carry):
        for i in range(n):
            taps = jnp.broadcast_to(krow_ref[pl.ds(j * n + i, 1), :], (CHUNK, CHUNK))
            shifted = pltpu.roll(taps, 0, 1, stride=1, stride_axis=0)
            t_ref[pl.ds(pl.multiple_of(j * CHUNK, CHUNK), CHUNK), i * CHUNK:(i + 1) * CHUNK] = (
                jnp.where(causal, shifted, 0.0).astype(BF16))
        return carry
    lax.fori_loop(0, n, build, 0)

    u = u_ref[...]
    ub = u.astype(BF16)

    b = jnp.dot(ub, w_ref[...], preferred_element_type=F32)
    pos = lax.broadcasted_iota(jnp.int32, (n_chunks, 2 * STATE_DIM), 0) & (chunks_per_seq - 1)

    def shift_rows(x, k):
        return jnp.where(pos >= k, pltpu.roll(x, k, 0), 0.0)

    h = shift_rows(b, 1)
    for k in range(int(math.log2(chunks_per_seq))):
        hs = shift_rows(h, 1 << k)
        h = h + hs * ap1_ref[k:k + 1, :] + pltpu.roll(hs, STATE_DIM, 1) * ap2_ref[k:k + 1, :]

    y = jnp.dot(ub, t_ref[...], preferred_element_type=F32)
    y = y + jnp.dot(h.astype(BF16), v_ref[...], preferred_element_type=F32)
    y = y + d_ref[...] * u
    y_ref[...] = jax.nn.gelu(y)


def _s5_main(u_g, krow, w, v, ap1, ap2, d_rep, chunks_per_seq):
    g, n_chunks, width = u_g.shape
    assert chunks_per_seq & (chunks_per_seq - 1) == 0 and chunks_per_seq <= 1 << N_DOUBLINGS
    spec = lambda *shape: pl.BlockSpec((None,) + shape, lambda i: (i,) + (0,) * len(shape))
    return pl.pallas_call(
        functools.partial(_s5_kernel, chunks_per_seq=chunks_per_seq),
        grid=(g,),
        in_specs=[spec(n_chunks, width), spec(SSM_GROUP * SSM_GROUP, CHUNK),
                  spec(width, 2 * STATE_DIM), spec(2 * STATE_DIM, width),
                  spec(N_DOUBLINGS, 2 * STATE_DIM), spec(N_DOUBLINGS, 2 * STATE_DIM),
                  spec(1, width)],
        out_specs=spec(n_chunks, width),
        out_shape=jax.ShapeDtypeStruct((g, n_chunks, width), F32),
        scratch_shapes=[pltpu.VMEM((width, width), BF16)],
        compiler_params=_params(1),
        name="s5_main",
    )(u_g, krow, w, v, ap1, ap2, d_rep)


def _glu_kernel(y_ref, w_ref, b_ref, g_ref, o_ref):
    y = y_ref[...]
    z = jnp.dot(y.astype(BF16), w_ref[...], preferred_element_type=F32) + b_ref[...]
    ssm = y * jax.nn.sigmoid(z)
    o_ref[...] = (_rms(ssm) * g_ref[...]).astype(BF16)


def _glu(y2, w_glu_bf, b_glu, ssm_out_g):
    t, w = y2.shape
    tm = 1024
    return pl.pallas_call(
        _glu_kernel,
        grid=(t // tm,),
        in_specs=[pl.BlockSpec((tm, w), lambda m: (m, 0)),
                  pl.BlockSpec((w, w), lambda m: (0, 0)),
                  pl.BlockSpec((1, w), lambda m: (0, 0)),
                  pl.BlockSpec((1, w), lambda m: (0, 0))],
        out_specs=pl.BlockSpec((tm, w), lambda m: (m, 0)),
        out_shape=jax.ShapeDtypeStruct((t, w), BF16),
        compiler_params=_params(1),
        name="glu",
    )(y2, w_glu_bf, b_glu.reshape(1, w), ssm_out_g.reshape(1, w))


def _outproj_kernel(attn_ref, ssm_ref, x_ref, mod_ref, ag_ref, wa_ref, ws_ref, g2_ref,
                    x1_ref, h2_ref):
    attn_n = (_rms(attn_ref[...].astype(F32)) * ag_ref[...]).astype(BF16)
    mixed = jnp.dot(attn_n, wa_ref[...], preferred_element_type=F32)
    mixed = mixed + jnp.dot(ssm_ref[...], ws_ref[...], preferred_element_type=F32)
    gate1 = mod_ref[0, 2:3, :]
    shift2 = mod_ref[0, 3:4, :]
    scale2 = mod_ref[0, 4:5, :]
    x1 = x_ref[...] + gate1 * mixed
    x1_ref[...] = x1
    h2_ref[...] = (_rms(x1) * g2_ref[...] * (1.0 + scale2) + shift2).astype(BF16)


def _outproj(attn2, ssm2, x2, mod3, attn_out_g, w_out_bf, norm2_g, seq):
    t, d = x2.shape
    wa = attn2.shape[1]
    ws = ssm2.shape[1]
    tm = 512
    per_batch = seq // tm
    return pl.pallas_call(
        _outproj_kernel,
        grid=(t // tm,),
        in_specs=[pl.BlockSpec((tm, wa), lambda m: (m, 0)),
                  pl.BlockSpec((tm, ws), lambda m: (m, 0)),
                  pl.BlockSpec((tm, d), lambda m: (m, 0)),
                  pl.BlockSpec((1, N_MOD, d), lambda m: (m // per_batch, 0, 0)),
                  pl.BlockSpec((1, wa), lambda m: (0, 0)),
                  pl.BlockSpec((wa, d), lambda m: (0, 0)),
                  pl.BlockSpec((ws, d), lambda m: (1, 0)),
                  pl.BlockSpec((1, d), lambda m: (0, 0))],
        out_specs=[pl.BlockSpec((tm, d), lambda m: (m, 0)),
                   pl.BlockSpec((tm, d), lambda m: (m, 0))],
        out_shape=[jax.ShapeDtypeStruct((t, d), F32),
                   jax.ShapeDtypeStruct((t, d), BF16)],
        compiler_params=_params(1),
        name="outproj",
    )(attn2, ssm2, x2, mod3, attn_out_g.reshape(1, wa), w_out_bf, w_out_bf,
      norm2_g.reshape(1, d))


def _ffn_kernel(h_ref, w1_ref, w2_ref, x1_ref, mod_ref, o_ref):
    f = pl.program_id(1)

    @pl.when(f == 0)
    def _():
        o_ref[...] = jnp.zeros_like(o_ref)

    a = jnp.dot(h_ref[...], w1_ref[...], preferred_element_type=F32)
    a = jnp.square(jnp.maximum(a, 0.0)).astype(BF16)
    o_ref[...] += jnp.dot(a, w2_ref[...], preferred_element_type=F32)

    @pl.when(f == pl.num_programs(1) - 1)
    def _():
        o_ref[...] = x1_ref[...] + mod_ref[0, 5:6, :] * o_ref[...]


def _ffn(h2, w1_bf, w2_bf, x1, mod3, seq):
    t, d = x1.shape
    dff = w1_bf.shape[1]
    tm = 512
    tf = 1024
    per_batch = seq // tm
    return pl.pallas_call(
        _ffn_kernel,
        grid=(t // tm, dff // tf),
        in_specs=[pl.BlockSpec((tm, d), lambda m, f: (m, 0)),
                  pl.BlockSpec((d, tf), lambda m, f: (0, f)),
                  pl.BlockSpec((tf, d), lambda m, f: (f, 0)),
                  pl.BlockSpec((tm, d), lambda m, f: (m, 0)),
                  pl.BlockSpec((1, N_MOD, d), lambda m, f: (m // per_batch, 0, 0))],
        out_specs=pl.BlockSpec((tm, d), lambda m, f: (m, 0)),
        out_shape=jax.ShapeDtypeStruct((t, d), F32),
        compiler_params=_params(2),
        name="ffn",
    )(h2, w1_bf, w2_bf, x1, mod3)


def _layer(x, c, w_ada, b_ada, norm1_g, w_in, q_norm_g, k_norm_g, lam_re, lam_im, log_step,
           b_re, b_im, c_re, c_im, d_skip, w_glu, b_glu, attn_out_g, ssm_out_g, w_out,
           norm2_g, w_ff1, w_ff2):
    bsz, seq, d = x.shape
    t = bsz * seq
    n_groups = lam_re.shape[0]
    ssm_width = n_groups * SSM_GROUP
    chunks_per_seq = seq // CHUNK

    mod3 = _adaln(c, w_ada, b_ada).reshape(bsz, N_MOD, d)
    x2 = x.reshape(t, d)
    qkv, u = _inproj(x2, mod3, norm1_g, w_in.astype(BF16), q_norm_g, k_norm_g, seq)

    slopes = 2.0 ** (-8.0 * (jnp.arange(N_HEADS, dtype=F32) + 1.0) / N_HEADS)
    slopes = jnp.broadcast_to(slopes[:, None, None], (N_HEADS, 1, 2 * BAND))
    attn = _attention(qkv.reshape(bsz, seq, 3 * ATTN_WIDTH), slopes)

    krow, w_s, v_s, ap1, ap2 = _s5_params(lam_re, lam_im, log_step, b_re, b_im, c_re, c_im)
    u_g = u.reshape(bsz * chunks_per_seq, CHUNK, n_groups, SSM_GROUP).transpose(2, 0, 3, 1)
    u_g = u_g.reshape(n_groups, bsz * chunks_per_seq, SSM_GROUP * CHUNK)
    d_rep = jnp.repeat(d_skip.reshape(n_groups, SSM_GROUP), CHUNK, axis=1)
    y_g = _s5_main(u_g, krow, w_s, v_s, ap1, ap2, d_rep.reshape(n_groups, 1, SSM_GROUP * CHUNK),
                   chunks_per_seq)
    y2 = y_g.reshape(n_groups, bsz * chunks_per_seq, SSM_GROUP, CHUNK).transpose(1, 3, 0, 2)
    y2 = y2.reshape(t, ssm_width)

    ssm = _glu(y2, w_glu.astype(BF16), b_glu, ssm_out_g)
    x1, h2 = _outproj(attn.reshape(t, ATTN_WIDTH), ssm, x2, mod3, attn_out_g,
                      w_out.astype(BF16), norm2_g, seq)
    out = _ffn(h2, w_ff1.astype(BF16), w_ff2.astype(BF16), x1, mod3, seq)
    return out.reshape(bsz, seq, d)


def kernel(x, c, w_ada, b_ada, norm1_g, w_in, q_norm_g, k_norm_g, lam_re, lam_im, log_step,
           b_re, b_im, c_re, c_im, d_skip, w_glu, b_glu, attn_out_g, ssm_out_g, w_out,
           norm2_g, w_ff1, w_ff2):
    for l in range(w_ada.shape[0]):
        x = _layer(x, c, w_ada[l], b_ada[l], norm1_g[l], w_in[l], q_norm_g[l], k_norm_g[l],
                   lam_re[l], lam_im[l], log_step[l], b_re[l], b_im[l], c_re[l], c_im[l],
                   d_skip[l], w_glu[l], b_glu[l], attn_out_g[l], ssm_out_g[l], w_out[l],
                   norm2_g[l], w_ff1[l], w_ff2[l])
    return x
```

```python
import functools
import math

import jax
import jax.numpy as jnp
from jax import lax
from jax.experimental import pallas as pl
from jax.experimental.pallas import tpu as pltpu

F32 = jnp.float32
BF16 = jnp.bfloat16

HEAD_DIM = 128
N_HEADS = 8
ATTN_WIDTH = N_HEADS * HEAD_DIM
SSM_GROUP = 16
STATE_DIM = 64
CHUNK = 128
BAND = 128
DILATIONS = (1, 4, 16)
N_MOD = 6
EPS = 1e-6
NEG = -1e30
LOG2E = math.log2(math.e)

V7X_VMEM_BYTES = 64 * 1024 * 1024
VMEM_LIMIT = V7X_VMEM_BYTES - 8 * 1024 * 1024


def _params(n_axes):
    return pltpu.CompilerParams(
        dimension_semantics=("arbitrary",) * n_axes, vmem_limit_bytes=VMEM_LIMIT)


def _rms(x, eps=EPS):
    return x * lax.rsqrt(jnp.mean(x * x, axis=-1, keepdims=True) + eps)


def _adaln_kernel(c_ref, w_ref, b_ref, o_ref):
    c = c_ref[...]
    s = (c * jax.nn.sigmoid(c)).astype(BF16)
    o_ref[...] = jnp.dot(s, w_ref[...].astype(BF16), preferred_element_type=F32) + b_ref[...]


def _adaln(c, w_ada, b_ada):
    bsz, d = c.shape
    n_out = w_ada.shape[1]
    rows = 8
    tn = 1024
    c8 = jnp.zeros((rows, d), F32).at[:bsz].set(c)
    out = pl.pallas_call(
        _adaln_kernel,
        grid=(n_out // tn,),
        in_specs=[pl.BlockSpec((rows, d), lambda n: (0, 0)),
                  pl.BlockSpec((d, tn), lambda n: (0, n)),
                  pl.BlockSpec((1, tn), lambda n: (0, n))],
        out_specs=pl.BlockSpec((rows, tn), lambda n: (0, n)),
        out_shape=jax.ShapeDtypeStruct((rows, n_out), F32),
        compiler_params=_params(1),
        name="adaln",
    )(c8, w_ada, b_ada.reshape(1, n_out))
    return out[:bsz]


STAGE_PAD = 4


def _inproj_kernel(x_ref, mod_ref, g1_ref, w_ref, wut_ref, qg_ref, kg_ref, qkv_ref, u_ref,
                   h_ref, ustage):
    n = pl.program_id(1)
    tm = h_ref.shape[0]

    @pl.when(n == 0)
    def _():
        shift = mod_ref[0, 0:1, :]
        scale = mod_ref[0, 1:2, :]
        y = _rms(x_ref[...]) * g1_ref[...]
        h_ref[...] = (y * (1.0 + scale) + shift).astype(BF16)

    def head_norm(gain):
        for pair in range(N_HEADS // 2):
            c0 = pair * 2 * HEAD_DIM
            acc = jnp.dot(h_ref[...], w_ref[:, c0:c0 + 2 * HEAD_DIM], preferred_element_type=F32)
            for hd in range(2):
                sl = slice(hd * HEAD_DIM, (hd + 1) * HEAD_DIM)
                qkv_ref[:, c0 + hd * HEAD_DIM:c0 + (hd + 1) * HEAD_DIM] = (
                    _rms(acc[:, sl]) * gain).astype(BF16)

    @pl.when(n == 0)
    def _():
        head_norm(qg_ref[...] * (HEAD_DIM ** -0.5 * LOG2E))

    @pl.when(n == 1)
    def _():
        head_norm(kg_ref[...])

    @pl.when(n == 2)
    def _():
        qkv_ref[...] = jnp.dot(h_ref[...], w_ref[...], preferred_element_type=F32).astype(BF16)

    @pl.when(n == 3)
    def _():
        width = wut_ref.shape[0]
        pitch = width + STAGE_PAD
        sub = 2 * CHUNK
        for c2 in range(tm // sub):
            ut = lax.dot_general(wut_ref[...], h_ref[c2 * sub:(c2 + 1) * sub, :],
                                 (((1,), (1,)), ((), ())), preferred_element_type=F32)
            for cc in range(sub // CHUNK):
                c = c2 * (sub // CHUNK) + cc
                ustage[c * pitch:c * pitch + width, :] = ut[:, cc * CHUNK:(cc + 1) * CHUNK]

        def regroup(g, carry):
            for j in range(SSM_GROUP):
                u_ref[g, :, j * CHUNK:(j + 1) * CHUNK] = (
                    ustage[pl.ds(g * SSM_GROUP + j, tm // CHUNK, stride=pitch), :])
            return carry
        lax.fori_loop(0, width // SSM_GROUP, regroup, 0)


def _inproj(x2, mod3, norm1_g, w_qkv_bf, w_ut_bf, q_norm_g, k_norm_g, seq):
    t, d = x2.shape
    tm = 1024
    tn = ATTN_WIDTH
    width = w_ut_bf.shape[0]
    n_groups = width // SSM_GROUP
    assert w_qkv_bf.shape[1] == 3 * tn and seq % tm == 0 and tm // CHUNK == 8
    per_batch = seq // tm
    return pl.pallas_call(
        _inproj_kernel,
        grid=(t // tm, 4),
        in_specs=[pl.BlockSpec((tm, d), lambda m, n: (m, 0)),
                  pl.BlockSpec((1, N_MOD, d), lambda m, n: (m // per_batch, 0, 0)),
                  pl.BlockSpec((1, d), lambda m, n: (0, 0)),
                  pl.BlockSpec((d, tn), lambda m, n: (0, jnp.minimum(n, 2))),
                  pl.BlockSpec((width, d), lambda m, n: (0, 0), pipeline_mode=pl.Buffered(1)),
                  pl.BlockSpec((1, HEAD_DIM), lambda m, n: (0, 0)),
                  pl.BlockSpec((1, HEAD_DIM), lambda m, n: (0, 0))],
        out_specs=[pl.BlockSpec((tm, tn), lambda m, n: (m, jnp.minimum(n, 2))),
                   pl.BlockSpec((n_groups, tm // CHUNK, SSM_GROUP * CHUNK), lambda m, n: (0, m, 0))],
        out_shape=[jax.ShapeDtypeStruct((t, 3 * tn), BF16),
                   jax.ShapeDtypeStruct((n_groups, t // CHUNK, SSM_GROUP * CHUNK), F32)],
        scratch_shapes=[pltpu.VMEM((tm, d), BF16),
                        pltpu.VMEM((tm // CHUNK * (width + STAGE_PAD), CHUNK), F32)],
        compiler_params=_params(2),
        name="inproj",
    )(x2, mod3, norm1_g.reshape(1, d), w_qkv_bf, w_ut_bf, q_norm_g.reshape(1, HEAD_DIM),
      k_norm_g.reshape(1, HEAD_DIM))


ATTN_SUPER = BAND * max(DILATIONS)
BLOCKS_PER_SUPER = ATTN_SUPER // BAND


def _attn_kernel(slope_ref, q_ref, k_ref, v_ref, o_ref,
                 stage, stage4, q4, q16, k1, k4, k16, v1, v4, v16,
                 bias_ref, oacc, mstat, lstat, *, seq):
    assert DILATIONS == (1, 4, 16)
    n_pat = len(DILATIONS)
    n4 = ATTN_SUPER // 4
    n16 = ATTN_SUPER // 16

    def deinterleave(src, a1, a4, a16, pad):
        def chunk(c, carry):
            r0 = pl.multiple_of(c * ATTN_SUPER, ATTN_SUPER)
            x = src[pl.ds(r0, ATTN_SUPER), :]
            if a1 is not None:
                a1[0, pl.ds(pad + r0, ATTN_SUPER), :] = x
            stage[...] = x.astype(F32)
            for r in range(4):
                y = stage[pl.ds(r, n4, stride=4), :]
                stage4[r * n4:(r + 1) * n4, :] = y
                a4[r, pl.ds(pad + pl.multiple_of(c * n4, n4), n4), :] = y.astype(BF16)
            for r in range(4):
                for a in range(4):
                    z = stage4[pl.ds(r * n4 + a, n16, stride=4), :]
                    a16[4 * a + r, pl.ds(pad + pl.multiple_of(c * n16, n16), n16), :] = (
                        z.astype(BF16))
            return carry
        lax.fori_loop(0, seq // ATTN_SUPER, chunk, 0)

    for buf in (k1, k4, k16, v1, v4, v16):
        buf[:, 0:BAND, :] = jnp.zeros((buf.shape[0], BAND, HEAD_DIM), BF16)
    deinterleave(q_ref, None, q4, q16, 0)
    deinterleave(k_ref, k1, k4, k16, BAND)
    deinterleave(v_ref, v1, v4, v16, BAND)

    qi = lax.broadcasted_iota(jnp.int32, (BAND, 2 * BAND), 0)
    ki = lax.broadcasted_iota(jnp.int32, (BAND, 2 * BAND), 1)
    steps = qi - ki + BAND
    valid = (steps >= 0) & (steps <= BAND)
    slope = slope_ref[...]
    for p, d in enumerate(DILATIONS):
        bias = jnp.where(valid, -(slope * (float(d) * LOG2E)) * steps.astype(F32), NEG)
        bias_ref[2 * p + 1] = bias
        bias_ref[2 * p] = jnp.where(ki >= BAND, bias, NEG)

    q_streams = (None, q4, q16)
    k_streams = (k1, k4, k16)
    v_streams = (v1, v4, v16)
    ones = jnp.ones((2 * BAND, HEAD_DIM), BF16)

    def rows(start, size, d):
        return pl.ds(start, size) if d == 1 else pl.ds(start, size, stride=d)

    def block(p, d, r, j, local):
        row0 = pl.multiple_of(j * BAND, BAND)
        if d == 1:
            q = q_ref[pl.ds(row0, BAND), :]
        else:
            q = q_streams[p][r, pl.ds(row0, BAND), :]
        k2 = k_streams[p][r, pl.ds(row0, 2 * BAND), :]
        v2 = v_streams[p][r, pl.ds(row0, 2 * BAND), :]
        s = lax.dot_general(q, k2, (((1,), (1,)), ((), ())), preferred_element_type=F32)
        has_prev = jnp.minimum(j, 1)
        s = s + bias_ref[2 * p + has_prev]
        m = jnp.max(s, axis=-1, keepdims=True)
        e = jnp.exp2(s - m).astype(BF16)
        acc = jnp.dot(e, jnp.concatenate([v2, ones], axis=1), preferred_element_type=F32)
        oacc[p, rows(local, BAND, d), :] = acc[:, :HEAD_DIM]
        lstat[p, rows(local, BAND, d), :] = acc[:, HEAD_DIM:]
        mstat[p, rows(local, BAND, d), :] = jnp.broadcast_to(m, (BAND, HEAD_DIM))

    merge_rows = 256

    def super_block(sb, carry):
        p0 = sb * ATTN_SUPER

        def blocks(idx, c2):
            for p, d in enumerate(DILATIONS):
                r = idx & (d - 1)
                jl = idx >> int(math.log2(d))
                block(p, d, r, sb * (BLOCKS_PER_SUPER // d) + jl, r + BAND * d * jl)
            return c2
        lax.fori_loop(0, BLOCKS_PER_SUPER, blocks, 0, unroll=4)

        def merge(c, c2):
            r = pl.multiple_of(c * merge_rows, merge_rows)
            ms = [mstat[p, pl.ds(r, merge_rows), :] for p in range(n_pat)]
            mx = functools.reduce(jnp.maximum, ms)
            num = jnp.zeros((merge_rows, HEAD_DIM), F32)
            den = jnp.zeros((merge_rows, HEAD_DIM), F32)
            for p in range(n_pat):
                w = jnp.exp2(ms[p] - mx)
                num = num + w * oacc[p, pl.ds(r, merge_rows), :]
                den = den + w * lstat[p, pl.ds(r, merge_rows), :]
            o_ref[pl.ds(pl.multiple_of(p0 + r, merge_rows), merge_rows), :] = (num / den).astype(BF16)
            return c2
        lax.fori_loop(0, ATTN_SUPER // merge_rows, merge, 0)
        return carry
    lax.fori_loop(0, seq // ATTN_SUPER, super_block, 0)


def _attention(qkv3, slopes):
    bsz, seq, _ = qkv3.shape
    assert seq % ATTN_SUPER == 0
    blk = lambda off: pl.BlockSpec((None, seq, HEAD_DIM), lambda b, h: (b, 0, off + h))
    stream = lambda d, pad: pltpu.VMEM((d, pad + seq // d, HEAD_DIM), BF16)
    stat = pltpu.VMEM((len(DILATIONS), ATTN_SUPER, HEAD_DIM), F32)
    return pl.pallas_call(
        functools.partial(_attn_kernel, seq=seq),
        grid=(bsz, N_HEADS),
        in_specs=[pl.BlockSpec((None, 1, 2 * BAND), lambda b, h: (h, 0, 0)),
                  blk(0), blk(N_HEADS), blk(2 * N_HEADS)],
        out_specs=pl.BlockSpec((None, seq, HEAD_DIM), lambda b, h: (b, 0, h)),
        out_shape=jax.ShapeDtypeStruct((bsz, seq, ATTN_WIDTH), BF16),
        scratch_shapes=[pltpu.VMEM((ATTN_SUPER, HEAD_DIM), F32),
                        pltpu.VMEM((ATTN_SUPER, HEAD_DIM), F32),
                        stream(4, 0), stream(16, 0),
                        stream(1, BAND), stream(4, BAND), stream(16, BAND),
                        stream(1, BAND), stream(4, BAND), stream(16, BAND),
                        pltpu.VMEM((2 * len(DILATIONS), BAND, 2 * BAND), F32),
                        stat, stat, stat],
        compiler_params=_params(2),
        name="attention",
    )(slopes, qkv3, qkv3, qkv3)


N_DOUBLINGS = 8


def _cmul(ar, ai, br, bi):
    return ar * br - ai * bi, ar * bi + ai * br


def _cpow(ar, ai, e, nbits):
    shape = jnp.broadcast_shapes(ar.shape, e.shape)
    pr = jnp.ones(shape, F32)
    pi = jnp.zeros(shape, F32)
    br, bi = ar, ai
    for k in range(nbits):
        nr, ni = _cmul(pr, pi, br, bi)
        sel = ((e >> k) & 1) == 1
        pr = jnp.where(sel, nr, pr)
        pi = jnp.where(sel, ni, pi)
        br, bi = _cmul(br, bi, br, bi)
    return pr, pi


def _zoh(lr, li, log_step):
    dt = jnp.exp(log_step)
    mag = jnp.exp(lr * dt)
    ar = mag * jnp.cos(li * dt)
    ai = mag * jnp.sin(li * dt)
    den = lr * lr + li * li
    cr = ((ar - 1.0) * lr + ai * li) / den
    ci = (ai * lr - (ar - 1.0) * li) / den
    return ar, ai, cr, ci


def _s5_params_kernel(lr_row, li_row, ls_row, lr_col, li_col, ls_col,
                      bt_re, bt_im, c_re, c_im, ct_re, ct_im,
                      krow_ref, w_ref, v_ref, ap1_ref, ap2_ref):
    p_dim = STATE_DIM
    ar, ai, cr, ci = _zoh(lr_row[...], li_row[...], ls_row[...])
    bbr, bbi = _cmul(cr, ci, bt_re[...], bt_im[...])
    s_idx = lax.broadcasted_iota(jnp.int32, (CHUNK, p_dim), 0)
    qr, qi = _cpow(ar, ai, CHUNK - 1 - s_idx, 7)
    for j in range(SSM_GROUP):
        wr, wi = _cmul(qr, qi, bbr[j:j + 1, :], bbi[j:j + 1, :])
        w_ref[j * CHUNK:(j + 1) * CHUNK, :] = jnp.concatenate([wr, wi], axis=-1).astype(BF16)
    mr, mi = ar, ai
    for _ in range(7):
        mr, mi = _cmul(mr, mi, mr, mi)
    for k in range(N_DOUBLINGS):
        ap1_ref[k:k + 1, :] = jnp.concatenate([mr, mr], axis=-1)
        ap2_ref[k:k + 1, :] = jnp.concatenate([-mi, mi], axis=-1)
        mr, mi = _cmul(mr, mi, mr, mi)

    acr, aci, _, _ = _zoh(lr_col[...], li_col[...], ls_col[...])
    t_idx = lax.broadcasted_iota(jnp.int32, (p_dim, CHUNK), 1)
    pr, pi = _cpow(acr, aci, t_idx, 7)
    cbr, cbi = [], []
    for j in range(SSM_GROUP):
        r_, i_ = _cmul(c_re[...], c_im[...], bbr[j:j + 1, :], bbi[j:j + 1, :])
        cbr.append(r_)
        cbi.append(i_)
    cbr = jnp.concatenate(cbr, axis=0)
    cbi = jnp.concatenate(cbi, axis=0)
    hi = lax.Precision.HIGHEST
    krow_ref[...] = (jnp.dot(cbr, pr, precision=hi, preferred_element_type=F32)
                     - jnp.dot(cbi, pi, precision=hi, preferred_element_type=F32))
    p1r, p1i = _cmul(pr, pi, acr, aci)
    ctr = ct_re[...]
    cti = ct_im[...]
    for i in range(SSM_GROUP):
        vr, vi = _cmul(ctr[:, i:i + 1], cti[:, i:i + 1], p1r, p1i)
        v_ref[0:p_dim, i * CHUNK:(i + 1) * CHUNK] = vr.astype(BF16)
        v_ref[p_dim:2 * p_dim, i * CHUNK:(i + 1) * CHUNK] = (-vi).astype(BF16)


def _s5_params(lam_re, lam_im, log_step, b_re, b_im, c_re, c_im):
    g, p = lam_re.shape
    n = SSM_GROUP
    row = lambda a: a.reshape(g, 1, p)
    col = lambda a: a.reshape(g, p, 1)
    ls = jnp.broadcast_to(log_step[:, None], (g, p))
    tr = lambda a: jnp.swapaxes(a, 1, 2)
    spec = lambda *shape: pl.BlockSpec((None,) + shape, lambda i: (i,) + (0,) * len(shape))
    return pl.pallas_call(
        _s5_params_kernel,
        grid=(g,),
        in_specs=[spec(1, p)] * 3 + [spec(p, 1)] * 3 + [spec(n, p)] * 4 + [spec(p, n)] * 2,
        out_specs=[spec(n * n, CHUNK), spec(n * CHUNK, 2 * p), spec(2 * p, n * CHUNK),
                   spec(N_DOUBLINGS, 2 * p), spec(N_DOUBLINGS, 2 * p)],
        out_shape=[jax.ShapeDtypeStruct((g, n * n, CHUNK), F32),
                   jax.ShapeDtypeStruct((g, n * CHUNK, 2 * p), BF16),
                   jax.ShapeDtypeStruct((g, 2 * p, n * CHUNK), BF16),
                   jax.ShapeDtypeStruct((g, N_DOUBLINGS, 2 * p), F32),
                   jax.ShapeDtypeStruct((g, N_DOUBLINGS, 2 * p), F32)],
        compiler_params=_params(1),
        name="s5_params",
    )(row(lam_re), row(lam_im), row(ls), col(lam_re), col(lam_im), col(ls),
      tr(b_re), tr(b_im), c_re, c_im, tr(c_re), tr(c_im))


def _s5_kernel(u_ref, krow_ref, w_ref, v_ref, ap1_ref, ap2_ref, d_ref, yt_ref, t_ref, ystage,
               *, chunks_per_seq):
    n = SSM_GROUP
    width = n * CHUNK
    n_chunks = u_ref.shape[0]

    s_idx = lax.broadcasted_iota(jnp.int32, (CHUNK, CHUNK), 0)
    t_idx = lax.broadcasted_iota(jnp.int32, (CHUNK, CHUNK), 1)
    causal = t_idx >= s_idx

    def build(j, carry):
        for i in range(n):
            taps = jnp.broadcast_to(krow_ref[pl.ds(j * n + i, 1), :], (CHUNK, CHUNK))
            shifted = pltpu.roll(taps, 0, 1, stride=1, stride_axis=0)
            t_ref[pl.ds(pl.multiple_of(j * CHUNK, CHUNK), CHUNK), i * CHUNK:(i + 1) * CHUNK] = (
                jnp.where(causal, shifted, 0.0).astype(BF16))
        return carry
    lax.fori_loop(0, n, build, 0)

    u = u_ref[...]
    ub = u.astype(BF16)

    b = jnp.dot(ub, w_ref[...], preferred_element_type=F32)
    pos = lax.broadcasted_iota(jnp.int32, (n_chunks, 2 * STATE_DIM), 0) & (chunks_per_seq - 1)

    def shift_rows(x, k):
        return jnp.where(pos >= k, pltpu.roll(x, k, 0), 0.0)

    h = shift_rows(b, 1)
    for k in range(int(math.log2(chunks_per_seq))):
        hs = shift_rows(h, 1 << k)
        h = h + hs * ap1_ref[k:k + 1, :] + pltpu.roll(hs, STATE_DIM, 1) * ap2_ref[k:k + 1, :]

    y = jnp.dot(ub, t_ref[...], preferred_element_type=F32)
    y = y + jnp.dot(h.astype(BF16), v_ref[...], preferred_element_type=F32)
    y = jax.nn.gelu(y + d_ref[...] * u)

    pitch = n_chunks + STAGE_PAD
    for i in range(n):
        ystage[i * pitch:i * pitch + n_chunks, :] = y[:, i * CHUNK:(i + 1) * CHUNK]
    for c in range(n_chunks):
        for i0 in range(0, n, 8):
            yt_ref[i0:i0 + 8, c * CHUNK:(c + 1) * CHUNK] = (
                ystage[pl.ds(i0 * pitch + c, 8, stride=pitch), :])


def _s5_main(u_g, krow, w, v, ap1, ap2, d_rep, chunks_per_seq):
    g, n_chunks, width = u_g.shape
    assert chunks_per_seq & (chunks_per_seq - 1) == 0 and chunks_per_seq <= 1 << N_DOUBLINGS
    spec = lambda *shape: pl.BlockSpec((None,) + shape, lambda i: (i,) + (0,) * len(shape))
    return pl.pallas_call(
        functools.partial(_s5_kernel, chunks_per_seq=chunks_per_seq),
        grid=(g,),
        in_specs=[spec(n_chunks, width), spec(SSM_GROUP * SSM_GROUP, CHUNK),
                  spec(width, 2 * STATE_DIM), spec(2 * STATE_DIM, width),
                  spec(N_DOUBLINGS, 2 * STATE_DIM), spec(N_DOUBLINGS, 2 * STATE_DIM),
                  spec(1, width)],
        out_specs=pl.BlockSpec((SSM_GROUP, n_chunks * CHUNK), lambda i: (i, 0)),
        out_shape=jax.ShapeDtypeStruct((g * SSM_GROUP, n_chunks * CHUNK), F32),
        scratch_shapes=[pltpu.VMEM((width, width), BF16),
                        pltpu.VMEM((SSM_GROUP * (n_chunks + STAGE_PAD), CHUNK), F32)],
        compiler_params=_params(1),
        name="s5_main",
    )(u_g, krow, w, v, ap1, ap2, d_rep)


def _glu_kernel(yt_ref, wt_ref, b_ref, g_ref, o_ref):
    y = yt_ref[...]
    z = jnp.dot(wt_ref[...], y.astype(BF16), preferred_element_type=F32) + b_ref[...]
    ssm = y * jax.nn.sigmoid(z)
    inv = lax.rsqrt(jnp.mean(ssm * ssm, axis=0, keepdims=True) + EPS)
    o_ref[...] = (ssm * inv * g_ref[...]).astype(BF16)


def _glu(yt, w_glu_t_bf, b_glu, ssm_out_g):
    w, t = yt.shape
    tn = 1024
    return pl.pallas_call(
        _glu_kernel,
        grid=(t // tn,),
        in_specs=[pl.BlockSpec((w, tn), lambda m: (0, m)),
                  pl.BlockSpec((w, w), lambda m: (0, 0)),
                  pl.BlockSpec((w, 1), lambda m: (0, 0)),
                  pl.BlockSpec((w, 1), lambda m: (0, 0))],
        out_specs=pl.BlockSpec((w, tn), lambda m: (0, m)),
        out_shape=jax.ShapeDtypeStruct((w, t), BF16),
        compiler_params=_params(1),
        name="glu",
    )(yt, w_glu_t_bf, b_glu.reshape(w, 1), ssm_out_g.reshape(w, 1))


def _outproj_kernel(attn_ref, ssmt_ref, x_ref, mod_ref, ag_ref, wa_ref, ws_ref, g2_ref,
                    x1_ref, h2_ref):
    attn_n = (_rms(attn_ref[...].astype(F32)) * ag_ref[...]).astype(BF16)
    mixed = jnp.dot(attn_n, wa_ref[...], preferred_element_type=F32)
    mixed = mixed + lax.dot_general(ssmt_ref[...], ws_ref[...], (((0,), (0,)), ((), ())),
                                    preferred_element_type=F32)
    gate1 = mod_ref[0, 2:3, :]
    shift2 = mod_ref[0, 3:4, :]
    scale2 = mod_ref[0, 4:5, :]
    x1 = x_ref[...] + gate1 * mixed
    x1_ref[...] = x1
    h2_ref[...] = (_rms(x1) * g2_ref[...] * (1.0 + scale2) + shift2).astype(BF16)


def _outproj(attn2, ssmt, x2, mod3, attn_out_g, w_out_bf, norm2_g, seq):
    t, d = x2.shape
    wa = attn2.shape[1]
    ws = ssmt.shape[0]
    tm = 512
    per_batch = seq // tm
    return pl.pallas_call(
        _outproj_kernel,
        grid=(t // tm,),
        in_specs=[pl.BlockSpec((tm, wa), lambda m: (m, 0)),
                  pl.BlockSpec((ws, tm), lambda m: (0, m)),
                  pl.BlockSpec((tm, d), lambda m: (m, 0)),
                  pl.BlockSpec((1, N_MOD, d), lambda m: (m // per_batch, 0, 0)),
                  pl.BlockSpec((1, wa), lambda m: (0, 0)),
                  pl.BlockSpec((wa, d), lambda m: (0, 0)),
                  pl.BlockSpec((ws, d), lambda m: (1, 0)),
                  pl.BlockSpec((1, d), lambda m: (0, 0))],
        out_specs=[pl.BlockSpec((tm, d), lambda m: (m, 0)),
                   pl.BlockSpec((tm, d), lambda m: (m, 0))],
        out_shape=[jax.ShapeDtypeStruct((t, d), F32),
                   jax.ShapeDtypeStruct((t, d), BF16)],
        compiler_params=_params(1),
        name="outproj",
    )(attn2, ssmt, x2, mod3, attn_out_g.reshape(1, wa), w_out_bf, w_out_bf,
      norm2_g.reshape(1, d))


def _ffn_kernel(h_ref, w1_ref, w2_ref, x1_ref, mod_ref, o_ref):
    f = pl.program_id(1)

    @pl.when(f == 0)
    def _():
        o_ref[...] = jnp.zeros_like(o_ref)

    a = jnp.dot(h_ref[...], w1_ref[...], preferred_element_type=F32)
    a = jnp.square(jnp.maximum(a, 0.0)).astype(BF16)
    o_ref[...] += jnp.dot(a, w2_ref[...], preferred_element_type=F32)

    @pl.when(f == pl.num_programs(1) - 1)
    def _():
        o_ref[...] = x1_ref[...] + mod_ref[0, 5:6, :] * o_ref[...]


def _ffn(h2, w1_bf, w2_bf, x1, mod3, seq):
    t, d = x1.shape
    dff = w1_bf.shape[1]
    tm = 512
    tf = 1024
    per_batch = seq // tm
    return pl.pallas_call(
        _ffn_kernel,
        grid=(t // tm, dff // tf),
        in_specs=[pl.BlockSpec((tm, d), lambda m, f: (m, 0)),
                  pl.BlockSpec((d, tf), lambda m, f: (0, f)),
                  pl.BlockSpec((tf, d), lambda m, f: (f, 0)),
                  pl.BlockSpec((tm, d), lambda m, f: (m, 0)),
                  pl.BlockSpec((1, N_MOD, d), lambda m, f: (m // per_batch, 0, 0))],
        out_specs=pl.BlockSpec((tm, d), lambda m, f: (m, 0)),
        out_shape=jax.ShapeDtypeStruct((t, d), F32),
        compiler_params=_params(2),
        name="ffn",
    )(h2, w1_bf, w2_bf, x1, mod3)


def _layer(x, c, w_ada, b_ada, norm1_g, w_in, q_norm_g, k_norm_g, lam_re, lam_im, log_step,
           b_re, b_im, c_re, c_im, d_skip, w_glu, b_glu, attn_out_g, ssm_out_g, w_out,
           norm2_g, w_ff1, w_ff2):
    bsz, seq, d = x.shape
    t = bsz * seq
    n_groups = lam_re.shape[0]
    chunks_per_seq = seq // CHUNK

    mod3 = _adaln(c, w_ada, b_ada).reshape(bsz, N_MOD, d)
    x2 = x.reshape(t, d)
    w_in_bf = w_in.astype(BF16)
    qkv, u_g = _inproj(x2, mod3, norm1_g, w_in_bf[:, :3 * ATTN_WIDTH], w_in_bf[:, 3 * ATTN_WIDTH:].T,
                       q_norm_g, k_norm_g, seq)

    slopes = 2.0 ** (-8.0 * (jnp.arange(N_HEADS, dtype=F32) + 1.0) / N_HEADS)
    slopes = jnp.broadcast_to(slopes[:, None, None], (N_HEADS, 1, 2 * BAND))
    attn = _attention(qkv.reshape(bsz, seq, 3 * ATTN_WIDTH), slopes)

    krow, w_s, v_s, ap1, ap2 = _s5_params(lam_re, lam_im, log_step, b_re, b_im, c_re, c_im)
    d_rep = jnp.repeat(d_skip.reshape(n_groups, SSM_GROUP), CHUNK, axis=1)
    yt = _s5_main(u_g, krow, w_s, v_s, ap1, ap2, d_rep.reshape(n_groups, 1, SSM_GROUP * CHUNK),
                  chunks_per_seq)
    ssm = _glu(yt, w_glu.T.astype(BF16), b_glu, ssm_out_g)
    x1, h2 = _outproj(attn.reshape(t, ATTN_WIDTH), ssm, x2, mod3, attn_out_g,
                      w_out.astype(BF16), norm2_g, seq)
    out = _ffn(h2, w_ff1.astype(BF16), w_ff2.astype(BF16), x1, mod3, seq)
    return out.reshape(bsz, seq, d)


def kernel(x, c, w_ada, b_ada, norm1_g, w_in, q_norm_g, k_norm_g, lam_re, lam_im, log_step,
           b_re, b_im, c_re, c_im, d_skip, w_glu, b_glu, attn_out_g, ssm_out_g, w_out,
           norm2_g, w_ff1, w_ff2):
    for l in range(w_ada.shape[0]):
        x = _layer(x, c, w_ada[l], b_ada[l], norm1_g[l], w_in[l], q_norm_g[l], k_norm_g[l],
                   lam_re[l], lam_im[l], log_step[l], b_re[l], b_im[l], c_re[l], c_im[l],
                   d_skip[l], w_glu[l], b_glu[l], attn_out_g[l], ssm_out_g[l], w_out[l],
                   norm2_g[l], w_ff1[l], w_ff2[l])
    return x
```

```python
import functools
import math

import jax
import jax.numpy as jnp
from jax import lax
from jax.experimental import pallas as pl
from jax.experimental.pallas import tpu as pltpu

F32 = jnp.float32
BF16 = jnp.bfloat16

HEAD_DIM = 128
N_HEADS = 8
ATTN_WIDTH = N_HEADS * HEAD_DIM
SSM_GROUP = 16
STATE_DIM = 64
CHUNK = 128
BAND = 128
DILATIONS = (1, 4, 16)
N_MOD = 6
EPS = 1e-6
NEG = -1e30
LOG2E = math.log2(math.e)

V7X_VMEM_BYTES = 64 * 1024 * 1024
VMEM_LIMIT = V7X_VMEM_BYTES - 8 * 1024 * 1024


def _params(n_axes):
    return pltpu.CompilerParams(
        dimension_semantics=("arbitrary",) * n_axes, vmem_limit_bytes=VMEM_LIMIT)


def _rms(x, eps=EPS):
    return x * lax.rsqrt(jnp.mean(x * x, axis=-1, keepdims=True) + eps)


def _adaln_kernel(c_ref, w_ref, b_ref, o_ref):
    c = c_ref[...]
    s = (c * jax.nn.sigmoid(c)).astype(BF16)
    o_ref[...] = jnp.dot(s, w_ref[...].astype(BF16), preferred_element_type=F32) + b_ref[...]


def _adaln(c, w_ada, b_ada):
    bsz, d = c.shape
    n_out = w_ada.shape[1]
    rows = 8
    tn = 1024
    c8 = jnp.zeros((rows, d), F32).at[:bsz].set(c)
    out = pl.pallas_call(
        _adaln_kernel,
        grid=(n_out // tn,),
        in_specs=[pl.BlockSpec((rows, d), lambda n: (0, 0)),
                  pl.BlockSpec((d, tn), lambda n: (0, n)),
                  pl.BlockSpec((1, tn), lambda n: (0, n))],
        out_specs=pl.BlockSpec((rows, tn), lambda n: (0, n)),
        out_shape=jax.ShapeDtypeStruct((rows, n_out), F32),
        compiler_params=_params(1),
        name="adaln",
    )(c8, w_ada, b_ada.reshape(1, n_out))
    return out[:bsz]


STAGE_PAD = 4


def _inproj_kernel(x_ref, mod_ref, g1_ref, w_ref, wut_ref, qg_ref, kg_ref, qkv_ref, u_ref,
                   h_ref, ustage):
    n = pl.program_id(1)
    tm = h_ref.shape[0]

    @pl.when(n == 0)
    def _():
        shift = mod_ref[0, 0:1, :]
        scale = mod_ref[0, 1:2, :]
        y = _rms(x_ref[...]) * g1_ref[...]
        h_ref[...] = (y * (1.0 + scale) + shift).astype(BF16)

    def head_norm(gain):
        for pair in range(N_HEADS // 2):
            c0 = pair * 2 * HEAD_DIM
            acc = jnp.dot(h_ref[...], w_ref[:, c0:c0 + 2 * HEAD_DIM], preferred_element_type=F32)
            for hd in range(2):
                sl = slice(hd * HEAD_DIM, (hd + 1) * HEAD_DIM)
                qkv_ref[:, c0 + hd * HEAD_DIM:c0 + (hd + 1) * HEAD_DIM] = (
                    _rms(acc[:, sl]) * gain).astype(BF16)

    @pl.when(n == 0)
    def _():
        head_norm(qg_ref[...] * (HEAD_DIM ** -0.5 * LOG2E))

    @pl.when(n == 1)
    def _():
        head_norm(kg_ref[...])

    @pl.when(n == 2)
    def _():
        qkv_ref[...] = jnp.dot(h_ref[...], w_ref[...], preferred_element_type=F32).astype(BF16)

    @pl.when(n == 3)
    def _():
        width = wut_ref.shape[0]
        pitch = width + STAGE_PAD
        sub = 2 * CHUNK
        for c2 in range(tm // sub):
            ut = lax.dot_general(wut_ref[...], h_ref[c2 * sub:(c2 + 1) * sub, :],
                                 (((1,), (1,)), ((), ())), preferred_element_type=F32)
            for cc in range(sub // CHUNK):
                c = c2 * (sub // CHUNK) + cc
                ustage[c * pitch:c * pitch + width, :] = ut[:, cc * CHUNK:(cc + 1) * CHUNK]

        def regroup(g, carry):
            for j in range(SSM_GROUP):
                u_ref[g, :, j * CHUNK:(j + 1) * CHUNK] = (
                    ustage[pl.ds(g * SSM_GROUP + j, tm // CHUNK, stride=pitch), :])
            return carry
        lax.fori_loop(0, width // SSM_GROUP, regroup, 0)


def _inproj(x2, mod3, norm1_g, w_qkv_bf, w_ut_bf, q_norm_g, k_norm_g, seq):
    t, d = x2.shape
    tm = 1024
    tn = ATTN_WIDTH
    width = w_ut_bf.shape[0]
    n_groups = width // SSM_GROUP
    assert w_qkv_bf.shape[1] == 3 * tn and seq % tm == 0 and tm // CHUNK == 8
    per_batch = seq // tm
    return pl.pallas_call(
        _inproj_kernel,
        grid=(t // tm, 4),
        in_specs=[pl.BlockSpec((tm, d), lambda m, n: (m, 0)),
                  pl.BlockSpec((1, N_MOD, d), lambda m, n: (m // per_batch, 0, 0)),
                  pl.BlockSpec((1, d), lambda m, n: (0, 0)),
                  pl.BlockSpec((d, tn), lambda m, n: (0, jnp.minimum(n, 2))),
                  pl.BlockSpec((width, d), lambda m, n: (0, 0), pipeline_mode=pl.Buffered(1)),
                  pl.BlockSpec((1, HEAD_DIM), lambda m, n: (0, 0)),
                  pl.BlockSpec((1, HEAD_DIM), lambda m, n: (0, 0))],
        out_specs=[pl.BlockSpec((tm, tn), lambda m, n: (m, jnp.minimum(n, 2))),
                   pl.BlockSpec((n_groups, tm // CHUNK, SSM_GROUP * CHUNK), lambda m, n: (0, m, 0))],
        out_shape=[jax.ShapeDtypeStruct((t, 3 * tn), BF16),
                   jax.ShapeDtypeStruct((n_groups, t // CHUNK, SSM_GROUP * CHUNK), F32)],
        scratch_shapes=[pltpu.VMEM((tm, d), BF16),
                        pltpu.VMEM((tm // CHUNK * (width + STAGE_PAD), CHUNK), F32)],
        compiler_params=_params(2),
        name="inproj",
    )(x2, mod3, norm1_g.reshape(1, d), w_qkv_bf, w_ut_bf, q_norm_g.reshape(1, HEAD_DIM),
      k_norm_g.reshape(1, HEAD_DIM))


ATTN_SUPER = BAND * max(DILATIONS)
BLOCKS_PER_SUPER = ATTN_SUPER // BAND


def _attn_kernel(slope_ref, q_ref, k_ref, v_ref, o_ref,
                 stage, stage4, q4, q16, k1, k4, k16, v1, v4, v16,
                 bias_ref, oacc, mstat, lstat, *, seq):
    assert DILATIONS == (1, 4, 16)
    n_pat = len(DILATIONS)
    n4 = ATTN_SUPER // 4
    n16 = ATTN_SUPER // 16

    def deinterleave(src, a1, a4, a16, pad):
        def chunk(c, carry):
            r0 = pl.multiple_of(c * ATTN_SUPER, ATTN_SUPER)
            x = src[pl.ds(r0, ATTN_SUPER), :]
            if a1 is not None:
                a1[0, pl.ds(pad + r0, ATTN_SUPER), :] = x
            stage[...] = x.astype(F32)
            for r in range(4):
                y = stage[pl.ds(r, n4, stride=4), :]
                stage4[r * n4:(r + 1) * n4, :] = y
                a4[r, pl.ds(pad + pl.multiple_of(c * n4, n4), n4), :] = y.astype(BF16)
            for r in range(4):
                for a in range(4):
                    z = stage4[pl.ds(r * n4 + a, n16, stride=4), :]
                    a16[4 * a + r, pl.ds(pad + pl.multiple_of(c * n16, n16), n16), :] = (
                        z.astype(BF16))
            return carry
        lax.fori_loop(0, seq // ATTN_SUPER, chunk, 0)

    for buf in (k1, k4, k16, v1, v4, v16):
        buf[:, 0:BAND, :] = jnp.zeros((buf.shape[0], BAND, HEAD_DIM), BF16)
    deinterleave(q_ref, None, q4, q16, 0)
    deinterleave(k_ref, k1, k4, k16, BAND)
    deinterleave(v_ref, v1, v4, v16, BAND)

    qi = lax.broadcasted_iota(jnp.int32, (BAND, 2 * BAND), 0)
    ki = lax.broadcasted_iota(jnp.int32, (BAND, 2 * BAND), 1)
    steps = qi - ki + BAND
    valid = (steps >= 0) & (steps <= BAND)
    slope = slope_ref[...]
    for p, d in enumerate(DILATIONS):
        bias = jnp.where(valid, -(slope * (float(d) * LOG2E)) * steps.astype(F32), NEG)
        bias_ref[2 * p + 1] = bias
        bias_ref[2 * p] = jnp.where(ki >= BAND, bias, NEG)

    q_streams = (None, q4, q16)
    k_streams = (k1, k4, k16)
    v_streams = (v1, v4, v16)
    ones = jnp.ones((2 * BAND, HEAD_DIM), BF16)

    def rows(start, size, d):
        return pl.ds(start, size) if d == 1 else pl.ds(start, size, stride=d)

    def block(p, d, r, j, local):
        row0 = pl.multiple_of(j * BAND, BAND)
        if d == 1:
            q = q_ref[pl.ds(row0, BAND), :]
        else:
            q = q_streams[p][r, pl.ds(row0, BAND), :]
        k2 = k_streams[p][r, pl.ds(row0, 2 * BAND), :]
        v2 = v_streams[p][r, pl.ds(row0, 2 * BAND), :]
        s = lax.dot_general(q, k2, (((1,), (1,)), ((), ())), preferred_element_type=F32)
        has_prev = jnp.minimum(j, 1)
        s = s + bias_ref[2 * p + has_prev]
        m = jnp.max(s, axis=-1, keepdims=True)
        e = jnp.exp2(s - m).astype(BF16)
        acc = jnp.dot(e, jnp.concatenate([v2, ones], axis=1), preferred_element_type=F32)
        oacc[p, rows(local, BAND, d), :] = acc[:, :HEAD_DIM]
        lstat[p, rows(local, BAND, d), :] = acc[:, HEAD_DIM:]
        mstat[p, rows(local, BAND, d), :] = jnp.broadcast_to(m, (BAND, HEAD_DIM))

    merge_rows = 256

    def super_block(sb, carry):
        p0 = sb * ATTN_SUPER

        def blocks(idx, c2):
            for p, d in enumerate(DILATIONS):
                r = idx & (d - 1)
                jl = idx >> int(math.log2(d))
                block(p, d, r, sb * (BLOCKS_PER_SUPER // d) + jl, r + BAND * d * jl)
            return c2
        lax.fori_loop(0, BLOCKS_PER_SUPER, blocks, 0, unroll=8)

        def merge(c, c2):
            r = pl.multiple_of(c * merge_rows, merge_rows)
            ms = [mstat[p, pl.ds(r, merge_rows), :] for p in range(n_pat)]
            mx = functools.reduce(jnp.maximum, ms)
            num = jnp.zeros((merge_rows, HEAD_DIM), F32)
            den = jnp.zeros((merge_rows, HEAD_DIM), F32)
            for p in range(n_pat):
                w = jnp.exp2(ms[p] - mx)
                num = num + w * oacc[p, pl.ds(r, merge_rows), :]
                den = den + w * lstat[p, pl.ds(r, merge_rows), :]
            o_ref[pl.ds(pl.multiple_of(p0 + r, merge_rows), merge_rows), :] = (num / den).astype(BF16)
            return c2
        lax.fori_loop(0, ATTN_SUPER // merge_rows, merge, 0)
        return carry
    lax.fori_loop(0, seq // ATTN_SUPER, super_block, 0)


def _attention(qkv3, slopes):
    bsz, seq, _ = qkv3.shape
    assert seq % ATTN_SUPER == 0
    blk = lambda off: pl.BlockSpec((None, seq, HEAD_DIM), lambda b, h: (b, 0, off + h))
    stream = lambda d, pad: pltpu.VMEM((d, pad + seq // d, HEAD_DIM), BF16)
    stat = pltpu.VMEM((len(DILATIONS), ATTN_SUPER, HEAD_DIM), F32)
    return pl.pallas_call(
        functools.partial(_attn_kernel, seq=seq),
        grid=(bsz, N_HEADS),
        in_specs=[pl.BlockSpec((None, 1, 2 * BAND), lambda b, h: (h, 0, 0)),
                  blk(0), blk(N_HEADS), blk(2 * N_HEADS)],
        out_specs=pl.BlockSpec((None, seq, HEAD_DIM), lambda b, h: (b, 0, h)),
        out_shape=jax.ShapeDtypeStruct((bsz, seq, ATTN_WIDTH), BF16),
        scratch_shapes=[pltpu.VMEM((ATTN_SUPER, HEAD_DIM), F32),
                        pltpu.VMEM((ATTN_SUPER, HEAD_DIM), F32),
                        stream(4, 0), stream(16, 0),
                        stream(1, BAND), stream(4, BAND), stream(16, BAND),
                        stream(1, BAND), stream(4, BAND), stream(16, BAND),
                        pltpu.VMEM((2 * len(DILATIONS), BAND, 2 * BAND), F32),
                        stat, stat, stat],
        compiler_params=_params(2),
        name="attention",
    )(slopes, qkv3, qkv3, qkv3)


N_DOUBLINGS = 8


def _cmul(ar, ai, br, bi):
    return ar * br - ai * bi, ar * bi + ai * br


def _cpow(ar, ai, e, nbits):
    shape = jnp.broadcast_shapes(ar.shape, e.shape)
    pr = jnp.ones(shape, F32)
    pi = jnp.zeros(shape, F32)
    br, bi = ar, ai
    for k in range(nbits):
        nr, ni = _cmul(pr, pi, br, bi)
        sel = ((e >> k) & 1) == 1
        pr = jnp.where(sel, nr, pr)
        pi = jnp.where(sel, ni, pi)
        br, bi = _cmul(br, bi, br, bi)
    return pr, pi


def _zoh(lr, li, log_step):
    dt = jnp.exp(log_step)
    mag = jnp.exp(lr * dt)
    ar = mag * jnp.cos(li * dt)
    ai = mag * jnp.sin(li * dt)
    den = lr * lr + li * li
    cr = ((ar - 1.0) * lr + ai * li) / den
    ci = (ai * lr - (ar - 1.0) * li) / den
    return ar, ai, cr, ci


def _s5_params_kernel(lr_row, li_row, ls_row, lr_col, li_col, ls_col,
                      bt_re, bt_im, c_re, c_im, ct_re, ct_im,
                      krow_ref, w_ref, v_ref, ap1_ref, ap2_ref):
    p_dim = STATE_DIM
    ar, ai, cr, ci = _zoh(lr_row[...], li_row[...], ls_row[...])
    bbr, bbi = _cmul(cr, ci, bt_re[...], bt_im[...])
    s_idx = lax.broadcasted_iota(jnp.int32, (CHUNK, p_dim), 0)
    qr, qi = _cpow(ar, ai, CHUNK - 1 - s_idx, 7)
    for j in range(SSM_GROUP):
        wr, wi = _cmul(qr, qi, bbr[j:j + 1, :], bbi[j:j + 1, :])
        w_ref[j * CHUNK:(j + 1) * CHUNK, :] = jnp.concatenate([wr, wi], axis=-1).astype(BF16)
    mr, mi = ar, ai
    for _ in range(7):
        mr, mi = _cmul(mr, mi, mr, mi)
    for k in range(N_DOUBLINGS):
        ap1_ref[k:k + 1, :] = jnp.concatenate([mr, mr], axis=-1)
        ap2_ref[k:k + 1, :] = jnp.concatenate([-mi, mi], axis=-1)
        mr, mi = _cmul(mr, mi, mr, mi)

    acr, aci, _, _ = _zoh(lr_col[...], li_col[...], ls_col[...])
    t_idx = lax.broadcasted_iota(jnp.int32, (p_dim, CHUNK), 1)
    pr, pi = _cpow(acr, aci, t_idx, 7)
    cbr, cbi = [], []
    for j in range(SSM_GROUP):
        r_, i_ = _cmul(c_re[...], c_im[...], bbr[j:j + 1, :], bbi[j:j + 1, :])
        cbr.append(r_)
        cbi.append(i_)
    cbr = jnp.concatenate(cbr, axis=0)
    cbi = jnp.concatenate(cbi, axis=0)
    hi = lax.Precision.HIGHEST
    krow_ref[...] = (jnp.dot(cbr, pr, precision=hi, preferred_element_type=F32)
                     - jnp.dot(cbi, pi, precision=hi, preferred_element_type=F32))
    p1r, p1i = _cmul(pr, pi, acr, aci)
    ctr = ct_re[...]
    cti = ct_im[...]
    for i in range(SSM_GROUP):
        vr, vi = _cmul(ctr[:, i:i + 1], cti[:, i:i + 1], p1r, p1i)
        v_ref[0:p_dim, i * CHUNK:(i + 1) * CHUNK] = vr.astype(BF16)
        v_ref[p_dim:2 * p_dim, i * CHUNK:(i + 1) * CHUNK] = (-vi).astype(BF16)


def _s5_params(lam_re, lam_im, log_step, b_re, b_im, c_re, c_im):
    g, p = lam_re.shape
    n = SSM_GROUP
    row = lambda a: a.reshape(g, 1, p)
    col = lambda a: a.reshape(g, p, 1)
    ls = jnp.broadcast_to(log_step[:, None], (g, p))
    tr = lambda a: jnp.swapaxes(a, 1, 2)
    spec = lambda *shape: pl.BlockSpec((None,) + shape, lambda i: (i,) + (0,) * len(shape))
    return pl.pallas_call(
        _s5_params_kernel,
        grid=(g,),
        in_specs=[spec(1, p)] * 3 + [spec(p, 1)] * 3 + [spec(n, p)] * 4 + [spec(p, n)] * 2,
        out_specs=[spec(n * n, CHUNK), spec(n * CHUNK, 2 * p), spec(2 * p, n * CHUNK),
                   spec(N_DOUBLINGS, 2 * p), spec(N_DOUBLINGS, 2 * p)],
        out_shape=[jax.ShapeDtypeStruct((g, n * n, CHUNK), F32),
                   jax.ShapeDtypeStruct((g, n * CHUNK, 2 * p), BF16),
                   jax.ShapeDtypeStruct((g, 2 * p, n * CHUNK), BF16),
                   jax.ShapeDtypeStruct((g, N_DOUBLINGS, 2 * p), F32),
                   jax.ShapeDtypeStruct((g, N_DOUBLINGS, 2 * p), F32)],
        compiler_params=_params(1),
        name="s5_params",
    )(row(lam_re), row(lam_im), row(ls), col(lam_re), col(lam_im), col(ls),
      tr(b_re), tr(b_im), c_re, c_im, tr(c_re), tr(c_im))


GROUPS_PER_STEP = 2
T_COLS = 4 * CHUNK


def _s5_kernel(u_ref, krow_ref, knext_ref, w_ref, v_ref, ap1_ref, ap2_ref, d_ref, yt_ref,
               t_ref, y_ref, ystage, *, chunks_per_seq):
    n = SSM_GROUP
    n_chunks = u_ref.shape[1]
    n_slabs = n * CHUNK // T_COLS
    per_slab = T_COLS // CHUNK

    s_idx = lax.broadcasted_iota(jnp.int32, (CHUNK, CHUNK), 0)
    t_idx = lax.broadcasted_iota(jnp.int32, (CHUNK, CHUNK), 1)
    causal = t_idx >= s_idx
    pos = lax.broadcasted_iota(jnp.int32, (n_chunks, 2 * STATE_DIM), 0) & (chunks_per_seq - 1)

    def build_slab(taps_ref, slot, c):
        def rows_j(j, carry):
            for e in range(per_slab):
                row = j * n + c * per_slab + e
                taps = jnp.broadcast_to(taps_ref[pl.ds(row, 1), :], (CHUNK, CHUNK))
                shifted = pltpu.roll(taps, 0, 1, stride=1, stride_axis=0)
                t_ref[slot, c, pl.ds(pl.multiple_of(j * CHUNK, CHUNK), CHUNK),
                      e * CHUNK:(e + 1) * CHUNK] = jnp.where(causal, shifted, 0.0).astype(BF16)
            return carry
        lax.fori_loop(0, n, rows_j, 0, unroll=True)

    def shift_rows(x, k):
        return jnp.where(pos >= k, pltpu.roll(x, k, 0), 0.0)

    def group(q, next_taps_ref):
        u = u_ref[q]
        ub = u.astype(BF16)
        b = jnp.dot(ub, w_ref[q], preferred_element_type=F32)
        h = shift_rows(b, 1)
        for k in range(int(math.log2(chunks_per_seq))):
            hs = shift_rows(h, 1 << k)
            h = (h + hs * ap1_ref[q, k:k + 1, :]
                 + pltpu.roll(hs, STATE_DIM, 1) * ap2_ref[q, k:k + 1, :])
        hv = jnp.dot(h.astype(BF16), v_ref[q], preferred_element_type=F32)

        def slab(c, carry):
            y_ref[c] = jnp.dot(ub, t_ref[q, c], preferred_element_type=F32)
            build_slab(next_taps_ref, 1 - q, c)
            return carry
        lax.fori_loop(0, n_slabs, slab, 0)

        pitch = n_chunks + STAGE_PAD
        base = q * n * pitch
        for i in range(n):
            cols = slice(i * CHUNK, (i + 1) * CHUNK)
            e = i % per_slab
            yi = (y_ref[i // per_slab, :, e * CHUNK:(e + 1) * CHUNK] + hv[:, cols]
                  + d_ref[q, :, cols] * u[:, cols])
            ystage[base + i * pitch:base + i * pitch + n_chunks, :] = jax.nn.gelu(yi)
        for c in range(n_chunks):
            for i0 in range(0, n, 8):
                yt_ref[q * n + i0:q * n + i0 + 8, c * CHUNK:(c + 1) * CHUNK] = (
                    ystage[pl.ds(base + i0 * pitch + c, 8, stride=pitch), :])

    @pl.when(pl.program_id(0) == 0)
    def _():
        def first(c, carry):
            build_slab(krow_ref.at[0], 0, c)
            return carry
        lax.fori_loop(0, n_slabs, first, 0)

    group(0, krow_ref.at[1])
    group(1, knext_ref)


def _s5_main(u_g, krow, w, v, ap1, ap2, d_rep, chunks_per_seq):
    g, n_chunks, width = u_g.shape
    gs = GROUPS_PER_STEP
    assert chunks_per_seq & (chunks_per_seq - 1) == 0 and chunks_per_seq <= 1 << N_DOUBLINGS
    assert g % gs == 0 and gs == 2
    spec = lambda *shape: pl.BlockSpec((gs,) + shape, lambda i: (i,) + (0,) * len(shape))
    taps = (SSM_GROUP * SSM_GROUP, CHUNK)
    return pl.pallas_call(
        functools.partial(_s5_kernel, chunks_per_seq=chunks_per_seq),
        grid=(g // gs,),
        in_specs=[spec(n_chunks, width), spec(*taps),
                  pl.BlockSpec((None,) + taps, lambda i: (jnp.minimum(gs * i + gs, g - 1), 0, 0)),
                  spec(width, 2 * STATE_DIM), spec(2 * STATE_DIM, width),
                  spec(N_DOUBLINGS, 2 * STATE_DIM), spec(N_DOUBLINGS, 2 * STATE_DIM),
                  spec(1, width)],
        out_specs=pl.BlockSpec((gs * SSM_GROUP, n_chunks * CHUNK), lambda i: (i, 0)),
        out_shape=jax.ShapeDtypeStruct((g * SSM_GROUP, n_chunks * CHUNK), F32),
        scratch_shapes=[pltpu.VMEM((gs, width // T_COLS, width, T_COLS), BF16),
                        pltpu.VMEM((width // T_COLS, n_chunks, T_COLS), F32),
                        pltpu.VMEM((gs * SSM_GROUP * (n_chunks + STAGE_PAD), CHUNK), F32)],
        compiler_params=_params(1),
        name="s5_main",
    )(u_g, krow, krow, w, v, ap1, ap2, d_rep)


def _glu_kernel(yt_ref, wt_ref, b_ref, g_ref, o_ref):
    y = yt_ref[...]
    z = jnp.dot(wt_ref[...], y.astype(BF16), preferred_element_type=F32) + b_ref[...]
    ssm = y * jax.nn.sigmoid(z)
    inv = lax.rsqrt(jnp.mean(ssm * ssm, axis=0, keepdims=True) + EPS)
    o_ref[...] = (ssm * inv * g_ref[...]).astype(BF16)


def _glu(yt, w_glu_t_bf, b_glu, ssm_out_g):
    w, t = yt.shape
    tn = 1024
    return pl.pallas_call(
        _glu_kernel,
        grid=(t // tn,),
        in_specs=[pl.BlockSpec((w, tn), lambda m: (0, m)),
                  pl.BlockSpec((w, w), lambda m: (0, 0)),
                  pl.BlockSpec((w, 1), lambda m: (0, 0)),
                  pl.BlockSpec((w, 1), lambda m: (0, 0))],
        out_specs=pl.BlockSpec((w, tn), lambda m: (0, m)),
        out_shape=jax.ShapeDtypeStruct((w, t), BF16),
        compiler_params=_params(1),
        name="glu",
    )(yt, w_glu_t_bf, b_glu.reshape(w, 1), ssm_out_g.reshape(w, 1))


def _outproj_kernel(attn_ref, ssmt_ref, x_ref, mod_ref, ag_ref, wa_ref, ws_ref, g2_ref,
                    x1_ref, h2_ref):
    attn_n = (_rms(attn_ref[...].astype(F32)) * ag_ref[...]).astype(BF16)
    mixed = jnp.dot(attn_n, wa_ref[...], preferred_element_type=F32)
    mixed = mixed + lax.dot_general(ssmt_ref[...], ws_ref[...], (((0,), (0,)), ((), ())),
                                    preferred_element_type=F32)
    gate1 = mod_ref[0, 2:3, :]
    shift2 = mod_ref[0, 3:4, :]
    scale2 = mod_ref[0, 4:5, :]
    x1 = x_ref[...] + gate1 * mixed
    x1_ref[...] = x1
    h2_ref[...] = (_rms(x1) * g2_ref[...] * (1.0 + scale2) + shift2).astype(BF16)


def _outproj(attn2, ssmt, x2, mod3, attn_out_g, w_out_bf, norm2_g, seq):
    t, d = x2.shape
    wa = attn2.shape[1]
    ws = ssmt.shape[0]
    tm = 512
    per_batch = seq // tm
    return pl.pallas_call(
        _outproj_kernel,
        grid=(t // tm,),
        in_specs=[pl.BlockSpec((tm, wa), lambda m: (m, 0)),
                  pl.BlockSpec((ws, tm), lambda m: (0, m)),
                  pl.BlockSpec((tm, d), lambda m: (m, 0)),
                  pl.BlockSpec((1, N_MOD, d), lambda m: (m // per_batch, 0, 0)),
                  pl.BlockSpec((1, wa), lambda m: (0, 0)),
                  pl.BlockSpec((wa, d), lambda m: (0, 0)),
                  pl.BlockSpec((ws, d), lambda m: (1, 0)),
                  pl.BlockSpec((1, d), lambda m: (0, 0))],
        out_specs=[pl.BlockSpec((tm, d), lambda m: (m, 0)),
                   pl.BlockSpec((tm, d), lambda m: (m, 0))],
        out_shape=[jax.ShapeDtypeStruct((t, d), F32),
                   jax.ShapeDtypeStruct((t, d), BF16)],
        compiler_params=_params(1),
        name="outproj",
    )(attn2, ssmt, x2, mod3, attn_out_g.reshape(1, wa), w_out_bf, w_out_bf,
      norm2_g.reshape(1, d))


def _ffn_kernel(h_ref, w1_ref, w2_ref, x1_ref, mod_ref, o_ref):
    f = pl.program_id(1)

    @pl.when(f == 0)
    def _():
        o_ref[...] = jnp.zeros_like(o_ref)

    a = jnp.dot(h_ref[...], w1_ref[...], preferred_element_type=F32)
    a = jnp.square(jnp.maximum(a, 0.0)).astype(BF16)
    o_ref[...] += jnp.dot(a, w2_ref[...], preferred_element_type=F32)

    @pl.when(f == pl.num_programs(1) - 1)
    def _():
        o_ref[...] = x1_ref[...] + mod_ref[0, 5:6, :] * o_ref[...]


def _ffn(h2, w1_bf, w2_bf, x1, mod3, seq):
    t, d = x1.shape
    dff = w1_bf.shape[1]
    tm = 512
    tf = 1024
    per_batch = seq // tm
    return pl.pallas_call(
        _ffn_kernel,
        grid=(t // tm, dff // tf),
        in_specs=[pl.BlockSpec((tm, d), lambda m, f: (m, 0)),
                  pl.BlockSpec((d, tf), lambda m, f: (0, f)),
                  pl.BlockSpec((tf, d), lambda m, f: (f, 0)),
                  pl.BlockSpec((tm, d), lambda m, f: (m, 0)),
                  pl.BlockSpec((1, N_MOD, d), lambda m, f: (m // per_batch, 0, 0))],
        out_specs=pl.BlockSpec((tm, d), lambda m, f: (m, 0)),
        out_shape=jax.ShapeDtypeStruct((t, d), F32),
        compiler_params=_params(2),
        name="ffn",
    )(h2, w1_bf, w2_bf, x1, mod3)


def _layer(x, c, w_ada, b_ada, norm1_g, w_in, q_norm_g, k_norm_g, lam_re, lam_im, log_step,
           b_re, b_im, c_re, c_im, d_skip, w_glu, b_glu, attn_out_g, ssm_out_g, w_out,
           norm2_g, w_ff1, w_ff2):
    bsz, seq, d = x.shape
    t = bsz * seq
    n_groups = lam_re.shape[0]
    chunks_per_seq = seq // CHUNK

    mod3 = _adaln(c, w_ada, b_ada).reshape(bsz, N_MOD, d)
    x2 = x.reshape(t, d)
    w_in_bf = w_in.astype(BF16)
    qkv, u_g = _inproj(x2, mod3, norm1_g, w_in_bf[:, :3 * ATTN_WIDTH], w_in_bf[:, 3 * ATTN_WIDTH:].T,
                       q_norm_g, k_norm_g, seq)

    slopes = 2.0 ** (-8.0 * (jnp.arange(N_HEADS, dtype=F32) + 1.0) / N_HEADS)
    slopes = jnp.broadcast_to(slopes[:, None, None], (N_HEADS, 1, 2 * BAND))
    attn = _attention(qkv.reshape(bsz, seq, 3 * ATTN_WIDTH), slopes)

    krow, w_s, v_s, ap1, ap2 = _s5_params(lam_re, lam_im, log_step, b_re, b_im, c_re, c_im)
    d_rep = jnp.repeat(d_skip.reshape(n_groups, SSM_GROUP), CHUNK, axis=1)
    yt = _s5_main(u_g, krow, w_s, v_s, ap1, ap2, d_rep.reshape(n_groups, 1, SSM_GROUP * CHUNK),
                  chunks_per_seq)
    ssm = _glu(yt, w_glu.T.astype(BF16), b_glu, ssm_out_g)
    x1, h2 = _outproj(attn.reshape(t, ATTN_WIDTH), ssm, x2, mod3, attn_out_g,
                      w_out.astype(BF16), norm2_g, seq)
    out = _ffn(h2, w_ff1.astype(BF16), w_ff2.astype(BF16), x1, mod3, seq)
    return out.reshape(bsz, seq, d)


def kernel(x, c, w_ada, b_ada, norm1_g, w_in, q_norm_g, k_norm_g, lam_re, lam_im, log_step,
           b_re, b_im, c_re, c_im, d_skip, w_glu, b_glu, attn_out_g, ssm_out_g, w_out,
           norm2_g, w_ff1, w_ff2):
    for l in range(w_ada.shape[0]):
        x = _layer(x, c, w_ada[l], b_ada[l], norm1_g[l], w_in[l], q_norm_g[l], k_norm_g[l],
                   lam_re[l], lam_im[l], log_step[l], b_re[l], b_im[l], c_re[l], c_im[l],
                   d_skip[l], w_glu[l], b_glu[l], attn_out_g[l], ssm_out_g[l], w_out[l],
                   norm2_g[l], w_ff1[l], w_ff2[l])
    return x
```

```python
import functools
import math

import jax
import jax.numpy as jnp
from jax import lax
from jax.experimental import pallas as pl
from jax.experimental.pallas import tpu as pltpu

F32 = jnp.float32
BF16 = jnp.bfloat16

HEAD_DIM = 128
N_HEADS = 8
ATTN_WIDTH = N_HEADS * HEAD_DIM
SSM_GROUP = 16
STATE_DIM = 64
CHUNK = 128
BAND = 128
DILATIONS = (1, 4, 16)
N_MOD = 6
EPS = 1e-6
NEG = -1e30
LOG2E = math.log2(math.e)

V7X_VMEM_BYTES = 64 * 1024 * 1024
VMEM_LIMIT = V7X_VMEM_BYTES - 8 * 1024 * 1024


def _params(n_axes):
    return pltpu.CompilerParams(
        dimension_semantics=("arbitrary",) * n_axes, vmem_limit_bytes=VMEM_LIMIT)


def _rms(x, eps=EPS):
    return x * lax.rsqrt(jnp.mean(x * x, axis=-1, keepdims=True) + eps)


def _adaln_kernel(c_ref, w_ref, b_ref, o_ref):
    c = c_ref[...]
    s = (c * jax.nn.sigmoid(c)).astype(BF16)
    o_ref[...] = jnp.dot(s, w_ref[...].astype(BF16), preferred_element_type=F32) + b_ref[...]


def _adaln(c, w_ada, b_ada):
    bsz, d = c.shape
    n_out = w_ada.shape[1]
    rows = 8
    tn = 1024
    c8 = jnp.zeros((rows, d), F32).at[:bsz].set(c)
    out = pl.pallas_call(
        _adaln_kernel,
        grid=(n_out // tn,),
        in_specs=[pl.BlockSpec((rows, d), lambda n: (0, 0)),
                  pl.BlockSpec((d, tn), lambda n: (0, n)),
                  pl.BlockSpec((1, tn), lambda n: (0, n))],
        out_specs=pl.BlockSpec((rows, tn), lambda n: (0, n)),
        out_shape=jax.ShapeDtypeStruct((rows, n_out), F32),
        compiler_params=_params(1),
        name="adaln",
    )(c8, w_ada, b_ada.reshape(1, n_out))
    return out[:bsz]


STAGE_PAD = 4


def _inproj_kernel(x_ref, mod_ref, g1_ref, w_ref, wut_ref, qg_ref, kg_ref, qkv_ref, u_ref,
                   h_ref, ustage):
    n = pl.program_id(1)
    tm = h_ref.shape[0]

    @pl.when(n == 0)
    def _():
        shift = mod_ref[0, 0:1, :]
        scale = mod_ref[0, 1:2, :]
        y = _rms(x_ref[...]) * g1_ref[...]
        h_ref[...] = (y * (1.0 + scale) + shift).astype(BF16)

    def head_norm(gain):
        for pair in range(N_HEADS // 2):
            c0 = pair * 2 * HEAD_DIM
            acc = jnp.dot(h_ref[...], w_ref[:, c0:c0 + 2 * HEAD_DIM], preferred_element_type=F32)
            for hd in range(2):
                sl = slice(hd * HEAD_DIM, (hd + 1) * HEAD_DIM)
                qkv_ref[:, c0 + hd * HEAD_DIM:c0 + (hd + 1) * HEAD_DIM] = (
                    _rms(acc[:, sl]) * gain).astype(BF16)

    @pl.when(n == 0)
    def _():
        head_norm(qg_ref[...] * (HEAD_DIM ** -0.5 * LOG2E))

    @pl.when(n == 1)
    def _():
        head_norm(kg_ref[...])

    @pl.when(n == 2)
    def _():
        qkv_ref[...] = jnp.dot(h_ref[...], w_ref[...], preferred_element_type=F32).astype(BF16)

    @pl.when(n == 3)
    def _():
        width = wut_ref.shape[0]
        pitch = width + STAGE_PAD
        sub = 2 * CHUNK
        for c2 in range(tm // sub):
            ut = lax.dot_general(wut_ref[...], h_ref[c2 * sub:(c2 + 1) * sub, :],
                                 (((1,), (1,)), ((), ())), preferred_element_type=F32)
            for cc in range(sub // CHUNK):
                c = c2 * (sub // CHUNK) + cc
                ustage[c * pitch:c * pitch + width, :] = ut[:, cc * CHUNK:(cc + 1) * CHUNK]

        def regroup(g, carry):
            for j in range(SSM_GROUP):
                u_ref[g, :, j * CHUNK:(j + 1) * CHUNK] = (
                    ustage[pl.ds(g * SSM_GROUP + j, tm // CHUNK, stride=pitch), :])
            return carry
        lax.fori_loop(0, width // SSM_GROUP, regroup, 0)


def _inproj(x2, mod3, norm1_g, w_qkv_bf, w_ut_bf, q_norm_g, k_norm_g, seq):
    t, d = x2.shape
    tm = 1024
    tn = ATTN_WIDTH
    width = w_ut_bf.shape[0]
    n_groups = width // SSM_GROUP
    assert w_qkv_bf.shape[1] == 3 * tn and seq % tm == 0 and tm // CHUNK == 8
    per_batch = seq // tm
    return pl.pallas_call(
        _inproj_kernel,
        grid=(t // tm, 4),
        in_specs=[pl.BlockSpec((tm, d), lambda m, n: (m, 0)),
                  pl.BlockSpec((1, N_MOD, d), lambda m, n: (m // per_batch, 0, 0)),
                  pl.BlockSpec((1, d), lambda m, n: (0, 0)),
                  pl.BlockSpec((d, tn), lambda m, n: (0, jnp.minimum(n, 2))),
                  pl.BlockSpec((width, d), lambda m, n: (0, 0), pipeline_mode=pl.Buffered(1)),
                  pl.BlockSpec((1, HEAD_DIM), lambda m, n: (0, 0)),
                  pl.BlockSpec((1, HEAD_DIM), lambda m, n: (0, 0))],
        out_specs=[pl.BlockSpec((tm, tn), lambda m, n: (m, jnp.minimum(n, 2))),
                   pl.BlockSpec((n_groups, tm // CHUNK, SSM_GROUP * CHUNK), lambda m, n: (0, m, 0))],
        out_shape=[jax.ShapeDtypeStruct((t, 3 * tn), BF16),
                   jax.ShapeDtypeStruct((n_groups, t // CHUNK, SSM_GROUP * CHUNK), F32)],
        scratch_shapes=[pltpu.VMEM((tm, d), BF16),
                        pltpu.VMEM((tm // CHUNK * (width + STAGE_PAD), CHUNK), F32)],
        compiler_params=_params(2),
        name="inproj",
    )(x2, mod3, norm1_g.reshape(1, d), w_qkv_bf, w_ut_bf, q_norm_g.reshape(1, HEAD_DIM),
      k_norm_g.reshape(1, HEAD_DIM))


ATTN_SUPER = BAND * max(DILATIONS)
BLOCKS_PER_SUPER = ATTN_SUPER // BAND


def _attn_kernel(slope_ref, q_ref, k_ref, v_ref, o_ref,
                 stage, stage4, q4, q16, k1, k4, k16, v1, v4, v16,
                 bias_ref, oacc, mstat, lstat, *, seq):
    assert DILATIONS == (1, 4, 16)
    n_pat = len(DILATIONS)
    n4 = ATTN_SUPER // 4
    n16 = ATTN_SUPER // 16

    def deinterleave(src, a1, a4, a16, pad):
        def chunk(c, carry):
            r0 = pl.multiple_of(c * ATTN_SUPER, ATTN_SUPER)
            x = src[pl.ds(r0, ATTN_SUPER), :]
            if a1 is not None:
                a1[0, pl.ds(pad + r0, ATTN_SUPER), :] = x
            stage[...] = x.astype(F32)
            for r in range(4):
                y = stage[pl.ds(r, n4, stride=4), :]
                stage4[r * n4:(r + 1) * n4, :] = y
                a4[r, pl.ds(pad + pl.multiple_of(c * n4, n4), n4), :] = y.astype(BF16)
            for r in range(4):
                for a in range(4):
                    z = stage4[pl.ds(r * n4 + a, n16, stride=4), :]
                    a16[4 * a + r, pl.ds(pad + pl.multiple_of(c * n16, n16), n16), :] = (
                        z.astype(BF16))
            return carry
        lax.fori_loop(0, seq // ATTN_SUPER, chunk, 0)

    for buf in (k1, k4, k16, v1, v4, v16):
        buf[:, 0:BAND, :] = jnp.zeros((buf.shape[0], BAND, HEAD_DIM), BF16)
    deinterleave(q_ref, None, q4, q16, 0)
    deinterleave(k_ref, k1, k4, k16, BAND)
    deinterleave(v_ref, v1, v4, v16, BAND)

    qi = lax.broadcasted_iota(jnp.int32, (BAND, 2 * BAND), 0)
    ki = lax.broadcasted_iota(jnp.int32, (BAND, 2 * BAND), 1)
    steps = qi - ki + BAND
    valid = (steps >= 0) & (steps <= BAND)
    slope = slope_ref[...]
    for p, d in enumerate(DILATIONS):
        bias = jnp.where(valid, -(slope * (float(d) * LOG2E)) * steps.astype(F32), NEG)
        bias_ref[2 * p + 1] = bias
        bias_ref[2 * p] = jnp.where(ki >= BAND, bias, NEG)

    q_streams = (None, q4, q16)
    k_streams = (k1, k4, k16)
    v_streams = (v1, v4, v16)
    ones = jnp.ones((2 * BAND, HEAD_DIM), BF16)

    def rows(start, size, d):
        return pl.ds(start, size) if d == 1 else pl.ds(start, size, stride=d)

    def block(p, d, r, j, local):
        row0 = pl.multiple_of(j * BAND, BAND)
        if d == 1:
            q = q_ref[pl.ds(row0, BAND), :]
        else:
            q = q_streams[p][r, pl.ds(row0, BAND), :]
        k2 = k_streams[p][r, pl.ds(row0, 2 * BAND), :]
        v2 = v_streams[p][r, pl.ds(row0, 2 * BAND), :]
        s = lax.dot_general(q, k2, (((1,), (1,)), ((), ())), preferred_element_type=F32)
        has_prev = jnp.minimum(j, 1)
        s = s + bias_ref[2 * p + has_prev]
        m = jnp.max(s, axis=-1, keepdims=True)
        e = jnp.exp2(s - m).astype(BF16)
        acc = jnp.dot(e, jnp.concatenate([v2, ones], axis=1), preferred_element_type=F32)
        oacc[p, rows(local, BAND, d), :] = acc[:, :HEAD_DIM]
        lstat[p, rows(local, BAND, d), :] = acc[:, HEAD_DIM:]
        mstat[p, rows(local, BAND, d), :] = jnp.broadcast_to(m, (BAND, HEAD_DIM))

    merge_rows = 256

    def super_block(sb, carry):
        p0 = sb * ATTN_SUPER

        def blocks(idx, c2):
            for p, d in enumerate(DILATIONS):
                r = idx & (d - 1)
                jl = idx >> int(math.log2(d))
                block(p, d, r, sb * (BLOCKS_PER_SUPER // d) + jl, r + BAND * d * jl)
            return c2
        lax.fori_loop(0, BLOCKS_PER_SUPER, blocks, 0, unroll=8)

        def merge(c, c2):
            r = pl.multiple_of(c * merge_rows, merge_rows)
            ms = [mstat[p, pl.ds(r, merge_rows), :] for p in range(n_pat)]
            mx = functools.reduce(jnp.maximum, ms)
            num = jnp.zeros((merge_rows, HEAD_DIM), F32)
            den = jnp.zeros((merge_rows, HEAD_DIM), F32)
            for p in range(n_pat):
                w = jnp.exp2(ms[p] - mx)
                num = num + w * oacc[p, pl.ds(r, merge_rows), :]
                den = den + w * lstat[p, pl.ds(r, merge_rows), :]
            o_ref[pl.ds(pl.multiple_of(p0 + r, merge_rows), merge_rows), :] = (num / den).astype(BF16)
            return c2
        lax.fori_loop(0, ATTN_SUPER // merge_rows, merge, 0)
        return carry
    lax.fori_loop(0, seq // ATTN_SUPER, super_block, 0)


def _attention(qkv3, slopes):
    bsz, seq, _ = qkv3.shape
    assert seq % ATTN_SUPER == 0
    blk = lambda off: pl.BlockSpec((None, seq, HEAD_DIM), lambda b, h: (b, 0, off + h))
    stream = lambda d, pad: pltpu.VMEM((d, pad + seq // d, HEAD_DIM), BF16)
    stat = pltpu.VMEM((len(DILATIONS), ATTN_SUPER, HEAD_DIM), F32)
    return pl.pallas_call(
        functools.partial(_attn_kernel, seq=seq),
        grid=(bsz, N_HEADS),
        in_specs=[pl.BlockSpec((None, 1, 2 * BAND), lambda b, h: (h, 0, 0)),
                  blk(0), blk(N_HEADS), blk(2 * N_HEADS)],
        out_specs=pl.BlockSpec((None, seq, HEAD_DIM), lambda b, h: (b, 0, h)),
        out_shape=jax.ShapeDtypeStruct((bsz, seq, ATTN_WIDTH), BF16),
        scratch_shapes=[pltpu.VMEM((ATTN_SUPER, HEAD_DIM), F32),
                        pltpu.VMEM((ATTN_SUPER, HEAD_DIM), F32),
                        stream(4, 0), stream(16, 0),
                        stream(1, BAND), stream(4, BAND), stream(16, BAND),
                        stream(1, BAND), stream(4, BAND), stream(16, BAND),
                        pltpu.VMEM((2 * len(DILATIONS), BAND, 2 * BAND), F32),
                        stat, stat, stat],
        compiler_params=_params(2),
        name="attention",
    )(slopes, qkv3, qkv3, qkv3)


N_DOUBLINGS = 8


def _cmul(ar, ai, br, bi):
    return ar * br - ai * bi, ar * bi + ai * br


def _cpow(ar, ai, e, nbits):
    shape = jnp.broadcast_shapes(ar.shape, e.shape)
    pr = jnp.ones(shape, F32)
    pi = jnp.zeros(shape, F32)
    br, bi = ar, ai
    for k in range(nbits):
        nr, ni = _cmul(pr, pi, br, bi)
        sel = ((e >> k) & 1) == 1
        pr = jnp.where(sel, nr, pr)
        pi = jnp.where(sel, ni, pi)
        br, bi = _cmul(br, bi, br, bi)
    return pr, pi


def _zoh(lr, li, log_step):
    dt = jnp.exp(log_step)
    mag = jnp.exp(lr * dt)
    ar = mag * jnp.cos(li * dt)
    ai = mag * jnp.sin(li * dt)
    den = lr * lr + li * li
    cr = ((ar - 1.0) * lr + ai * li) / den
    ci = (ai * lr - (ar - 1.0) * li) / den
    return ar, ai, cr, ci


def _s5_params_kernel(lr_row, li_row, ls_row, lr_col, li_col, ls_col,
                      bt_re, bt_im, c_re, c_im, ct_re, ct_im,
                      krow_ref, w_ref, v_ref, ap1_ref, ap2_ref):
    p_dim = STATE_DIM
    ar, ai, cr, ci = _zoh(lr_row[...], li_row[...], ls_row[...])
    bbr, bbi = _cmul(cr, ci, bt_re[...], bt_im[...])
    s_idx = lax.broadcasted_iota(jnp.int32, (CHUNK, p_dim), 0)
    qr, qi = _cpow(ar, ai, CHUNK - 1 - s_idx, 7)
    for j in range(SSM_GROUP):
        wr, wi = _cmul(qr, qi, bbr[j:j + 1, :], bbi[j:j + 1, :])
        w_ref[j * CHUNK:(j + 1) * CHUNK, :] = jnp.concatenate([wr, wi], axis=-1).astype(BF16)
    mr, mi = ar, ai
    for _ in range(7):
        mr, mi = _cmul(mr, mi, mr, mi)
    for k in range(N_DOUBLINGS):
        ap1_ref[k:k + 1, :] = jnp.concatenate([mr, mr], axis=-1)
        ap2_ref[k:k + 1, :] = jnp.concatenate([-mi, mi], axis=-1)
        mr, mi = _cmul(mr, mi, mr, mi)

    acr, aci, _, _ = _zoh(lr_col[...], li_col[...], ls_col[...])
    t_idx = lax.broadcasted_iota(jnp.int32, (p_dim, CHUNK), 1)
    pr, pi = _cpow(acr, aci, t_idx, 7)
    cbr, cbi = [], []
    for j in range(SSM_GROUP):
        r_, i_ = _cmul(c_re[...], c_im[...], bbr[j:j + 1, :], bbi[j:j + 1, :])
        cbr.append(r_)
        cbi.append(i_)
    cbr = jnp.concatenate(cbr, axis=0)
    cbi = jnp.concatenate(cbi, axis=0)
    hi = lax.Precision.HIGHEST
    krow_ref[...] = (jnp.dot(cbr, pr, precision=hi, preferred_element_type=F32)
                     - jnp.dot(cbi, pi, precision=hi, preferred_element_type=F32))
    p1r, p1i = _cmul(pr, pi, acr, aci)
    ctr = ct_re[...]
    cti = ct_im[...]
    for i in range(SSM_GROUP):
        vr, vi = _cmul(ctr[:, i:i + 1], cti[:, i:i + 1], p1r, p1i)
        v_ref[0:p_dim, i * CHUNK:(i + 1) * CHUNK] = vr.astype(BF16)
        v_ref[p_dim:2 * p_dim, i * CHUNK:(i + 1) * CHUNK] = (-vi).astype(BF16)


def _s5_params(lam_re, lam_im, log_step, b_re, b_im, c_re, c_im):
    g, p = lam_re.shape
    n = SSM_GROUP
    row = lambda a: a.reshape(g, 1, p)
    col = lambda a: a.reshape(g, p, 1)
    ls = jnp.broadcast_to(log_step[:, None], (g, p))
    tr = lambda a: jnp.swapaxes(a, 1, 2)
    spec = lambda *shape: pl.BlockSpec((None,) + shape, lambda i: (i,) + (0,) * len(shape))
    return pl.pallas_call(
        _s5_params_kernel,
        grid=(g,),
        in_specs=[spec(1, p)] * 3 + [spec(p, 1)] * 3 + [spec(n, p)] * 4 + [spec(p, n)] * 2,
        out_specs=[spec(n * n, CHUNK), spec(n * CHUNK, 2 * p), spec(2 * p, n * CHUNK),
                   spec(N_DOUBLINGS, 2 * p), spec(N_DOUBLINGS, 2 * p)],
        out_shape=[jax.ShapeDtypeStruct((g, n * n, CHUNK), F32),
                   jax.ShapeDtypeStruct((g, n * CHUNK, 2 * p), BF16),
                   jax.ShapeDtypeStruct((g, 2 * p, n * CHUNK), BF16),
                   jax.ShapeDtypeStruct((g, N_DOUBLINGS, 2 * p), F32),
                   jax.ShapeDtypeStruct((g, N_DOUBLINGS, 2 * p), F32)],
        compiler_params=_params(1),
        name="s5_params",
    )(row(lam_re), row(lam_im), row(ls), col(lam_re), col(lam_im), col(ls),
      tr(b_re), tr(b_im), c_re, c_im, tr(c_re), tr(c_im))


GROUPS_PER_STEP = 2
T_COLS = 4 * CHUNK


def _s5_kernel(u_ref, krow_ref, knext_ref, w_ref, v_ref, ap1_ref, ap2_ref, d_ref, yt_ref,
               t_ref, ystage, *, chunks_per_seq):
    n = SSM_GROUP
    n_chunks = u_ref.shape[1]
    n_slabs = n * CHUNK // T_COLS
    per_slab = T_COLS // CHUNK

    s_idx = lax.broadcasted_iota(jnp.int32, (CHUNK, CHUNK), 0)
    t_idx = lax.broadcasted_iota(jnp.int32, (CHUNK, CHUNK), 1)
    causal = t_idx >= s_idx
    pos = lax.broadcasted_iota(jnp.int32, (n_chunks, 2 * STATE_DIM), 0) & (chunks_per_seq - 1)

    def build_slab(taps_ref, slot, c):
        def rows_j(j, carry):
            for e in range(per_slab):
                row = j * n + c * per_slab + e
                taps = jnp.broadcast_to(taps_ref[pl.ds(row, 1), :], (CHUNK, CHUNK))
                shifted = pltpu.roll(taps, 0, 1, stride=1, stride_axis=0)
                t_ref[slot, c, pl.ds(pl.multiple_of(j * CHUNK, CHUNK), CHUNK),
                      e * CHUNK:(e + 1) * CHUNK] = jnp.where(causal, shifted, 0.0).astype(BF16)
            return carry
        lax.fori_loop(0, n, rows_j, 0, unroll=True)

    def shift_rows(x, k):
        return jnp.where(pos >= k, pltpu.roll(x, k, 0), 0.0)

    def group(q, next_taps_ref):
        u = u_ref[q]
        ub = u.astype(BF16)
        b = jnp.dot(ub, w_ref[q], preferred_element_type=F32)
        h = shift_rows(b, 1)
        for k in range(int(math.log2(chunks_per_seq))):
            hs = shift_rows(h, 1 << k)
            h = (h + hs * ap1_ref[q, k:k + 1, :]
                 + pltpu.roll(hs, STATE_DIM, 1) * ap2_ref[q, k:k + 1, :])
        hv = jnp.dot(h.astype(BF16), v_ref[q], preferred_element_type=F32)

        def rebuild(c, carry):
            build_slab(next_taps_ref, 1 - q, c)
            return carry
        lax.fori_loop(0, n_slabs, rebuild, 0, unroll=True)

        pitch = n_chunks + STAGE_PAD
        base = q * n * pitch
        for c in range(n_slabs):
            yc = jnp.dot(ub, t_ref[q, c], preferred_element_type=F32)
            for e in range(per_slab):
                i = c * per_slab + e
                cols = slice(i * CHUNK, (i + 1) * CHUNK)
                yi = yc[:, e * CHUNK:(e + 1) * CHUNK] + hv[:, cols] + d_ref[q, :, cols] * u[:, cols]
                ystage[base + i * pitch:base + i * pitch + n_chunks, :] = jax.nn.gelu(yi)
        for c in range(n_chunks):
            for i0 in range(0, n, 8):
                yt_ref[q * n + i0:q * n + i0 + 8, c * CHUNK:(c + 1) * CHUNK] = (
                    ystage[pl.ds(base + i0 * pitch + c, 8, stride=pitch), :])

    @pl.when(pl.program_id(0) == 0)
    def _():
        def first(c, carry):
            build_slab(krow_ref.at[0], 0, c)
            return carry
        lax.fori_loop(0, n_slabs, first, 0)

    group(0, krow_ref.at[1])
    group(1, knext_ref)


def _s5_main(u_g, krow, w, v, ap1, ap2, d_rep, chunks_per_seq):
    g, n_chunks, width = u_g.shape
    gs = GROUPS_PER_STEP
    assert chunks_per_seq & (chunks_per_seq - 1) == 0 and chunks_per_seq <= 1 << N_DOUBLINGS
    assert g % gs == 0 and gs == 2
    spec = lambda *shape: pl.BlockSpec((gs,) + shape, lambda i: (i,) + (0,) * len(shape))
    taps = (SSM_GROUP * SSM_GROUP, CHUNK)
    return pl.pallas_call(
        functools.partial(_s5_kernel, chunks_per_seq=chunks_per_seq),
        grid=(g // gs,),
        in_specs=[spec(n_chunks, width), spec(*taps),
                  pl.BlockSpec((None,) + taps, lambda i: (jnp.minimum(gs * i + gs, g - 1), 0, 0)),
                  spec(width, 2 * STATE_DIM), spec(2 * STATE_DIM, width),
                  spec(N_DOUBLINGS, 2 * STATE_DIM), spec(N_DOUBLINGS, 2 * STATE_DIM),
                  spec(1, width)],
        out_specs=pl.BlockSpec((gs * SSM_GROUP, n_chunks * CHUNK), lambda i: (i, 0)),
        out_shape=jax.ShapeDtypeStruct((g * SSM_GROUP, n_chunks * CHUNK), F32),
        scratch_shapes=[pltpu.VMEM((gs, width // T_COLS, width, T_COLS), BF16),
                        pltpu.VMEM((gs * SSM_GROUP * (n_chunks + STAGE_PAD), CHUNK), F32)],
        compiler_params=_params(1),
        name="s5_main",
    )(u_g, krow, krow, w, v, ap1, ap2, d_rep)


def _glu_kernel(yt_ref, wt_ref, b_ref, g_ref, o_ref):
    y = yt_ref[...]
    z = jnp.dot(wt_ref[...], y.astype(BF16), preferred_element_type=F32) + b_ref[...]
    ssm = y * jax.nn.sigmoid(z)
    inv = lax.rsqrt(jnp.mean(ssm * ssm, axis=0, keepdims=True) + EPS)
    o_ref[...] = (ssm * inv * g_ref[...]).astype(BF16)


def _glu(yt, w_glu_t_bf, b_glu, ssm_out_g):
    w, t = yt.shape
    tn = 1024
    return pl.pallas_call(
        _glu_kernel,
        grid=(t // tn,),
        in_specs=[pl.BlockSpec((w, tn), lambda m: (0, m)),
                  pl.BlockSpec((w, w), lambda m: (0, 0)),
                  pl.BlockSpec((w, 1), lambda m: (0, 0)),
                  pl.BlockSpec((w, 1), lambda m: (0, 0))],
        out_specs=pl.BlockSpec((w, tn), lambda m: (0, m)),
        out_shape=jax.ShapeDtypeStruct((w, t), BF16),
        compiler_params=_params(1),
        name="glu",
    )(yt, w_glu_t_bf, b_glu.reshape(w, 1), ssm_out_g.reshape(w, 1))


OUT_COLS = 512


def _outproj_kernel(attn_ref, ssmt_ref, x_ref, mod_ref, ag_ref, wa_ref, ws_ref, g2_ref,
                    x1_ref, h2_ref):
    d = x_ref.shape[1]
    attn_n = (_rms(attn_ref[...].astype(F32)) * ag_ref[...]).astype(BF16)
    ssmt = ssmt_ref[...]
    sq = jnp.zeros((x_ref.shape[0], 1), F32)
    for c0 in range(0, d, OUT_COLS):
        cols = slice(c0, c0 + OUT_COLS)
        mixed = jnp.dot(attn_n, wa_ref[:, cols], preferred_element_type=F32)
        mixed = mixed + lax.dot_general(ssmt, ws_ref[:, cols], (((0,), (0,)), ((), ())),
                                        preferred_element_type=F32)
        x1 = x_ref[:, cols] + mod_ref[0, 2:3, cols] * mixed
        x1_ref[:, cols] = x1
        sq = sq + jnp.sum(x1 * x1, axis=-1, keepdims=True)
    inv = lax.rsqrt(sq * (1.0 / d) + EPS)
    for c0 in range(0, d, OUT_COLS):
        cols = slice(c0, c0 + OUT_COLS)
        gain = g2_ref[:, cols] * (1.0 + mod_ref[0, 4:5, cols])
        h2_ref[:, cols] = (x1_ref[:, cols] * inv * gain + mod_ref[0, 3:4, cols]).astype(BF16)


def _outproj(attn2, ssmt, x2, mod3, attn_out_g, w_out_bf, norm2_g, seq):
    t, d = x2.shape
    wa = attn2.shape[1]
    ws = ssmt.shape[0]
    tm = 512
    per_batch = seq // tm
    return pl.pallas_call(
        _outproj_kernel,
        grid=(t // tm,),
        in_specs=[pl.BlockSpec((tm, wa), lambda m: (m, 0)),
                  pl.BlockSpec((ws, tm), lambda m: (0, m)),
                  pl.BlockSpec((tm, d), lambda m: (m, 0)),
                  pl.BlockSpec((1, N_MOD, d), lambda m: (m // per_batch, 0, 0)),
                  pl.BlockSpec((1, wa), lambda m: (0, 0)),
                  pl.BlockSpec((wa, d), lambda m: (0, 0)),
                  pl.BlockSpec((ws, d), lambda m: (1, 0)),
                  pl.BlockSpec((1, d), lambda m: (0, 0))],
        out_specs=[pl.BlockSpec((tm, d), lambda m: (m, 0)),
                   pl.BlockSpec((tm, d), lambda m: (m, 0))],
        out_shape=[jax.ShapeDtypeStruct((t, d), F32),
                   jax.ShapeDtypeStruct((t, d), BF16)],
        compiler_params=_params(1),
        name="outproj",
    )(attn2, ssmt, x2, mod3, attn_out_g.reshape(1, wa), w_out_bf, w_out_bf,
      norm2_g.reshape(1, d))


def _ffn_kernel(h_ref, w1_ref, w2_ref, x1_ref, mod_ref, o_ref):
    f = pl.program_id(1)

    def partial_sum():
        a = jnp.dot(h_ref[...], w1_ref[...], preferred_element_type=F32)
        a = jnp.square(jnp.maximum(a, 0.0)).astype(BF16)
        return jnp.dot(a, w2_ref[...], preferred_element_type=F32)

    @pl.when(f == 0)
    def _():
        o_ref[...] = partial_sum()

    @pl.when(f > 0)
    def _():
        o_ref[...] += partial_sum()

    @pl.when(f == pl.num_programs(1) - 1)
    def _():
        o_ref[...] = x1_ref[...] + mod_ref[0, 5:6, :] * o_ref[...]


def _ffn(h2, w1_bf, w2_bf, x1, mod3, seq):
    t, d = x1.shape
    dff = w1_bf.shape[1]
    tm = 512
    tf = 1024
    per_batch = seq // tm
    return pl.pallas_call(
        _ffn_kernel,
        grid=(t // tm, dff // tf),
        in_specs=[pl.BlockSpec((tm, d), lambda m, f: (m, 0)),
                  pl.BlockSpec((d, tf), lambda m, f: (0, f)),
                  pl.BlockSpec((tf, d), lambda m, f: (f, 0)),
                  pl.BlockSpec((tm, d), lambda m, f: (m, 0)),
                  pl.BlockSpec((1, N_MOD, d), lambda m, f: (m // per_batch, 0, 0))],
        out_specs=pl.BlockSpec((tm, d), lambda m, f: (m, 0)),
        out_shape=jax.ShapeDtypeStruct((t, d), F32),
        compiler_params=_params(2),
        name="ffn",
    )(h2, w1_bf, w2_bf, x1, mod3)


def _layer(x, c, w_ada, b_ada, norm1_g, w_in, q_norm_g, k_norm_g, lam_re, lam_im, log_step,
           b_re, b_im, c_re, c_im, d_skip, w_glu, b_glu, attn_out_g, ssm_out_g, w_out,
           norm2_g, w_ff1, w_ff2):
    bsz, seq, d = x.shape
    t = bsz * seq
    n_groups = lam_re.shape[0]
    chunks_per_seq = seq // CHUNK

    mod3 = _adaln(c, w_ada, b_ada).reshape(bsz, N_MOD, d)
    x2 = x.reshape(t, d)
    w_in_bf = w_in.astype(BF16)
    qkv, u_g = _inproj(x2, mod3, norm1_g, w_in_bf[:, :3 * ATTN_WIDTH], w_in_bf[:, 3 * ATTN_WIDTH:].T,
                       q_norm_g, k_norm_g, seq)

    slopes = 2.0 ** (-8.0 * (jnp.arange(N_HEADS, dtype=F32) + 1.0) / N_HEADS)
    slopes = jnp.broadcast_to(slopes[:, None, None], (N_HEADS, 1, 2 * BAND))
    attn = _attention(qkv.reshape(bsz, seq, 3 * ATTN_WIDTH), slopes)

    krow, w_s, v_s, ap1, ap2 = _s5_params(lam_re, lam_im, log_step, b_re, b_im, c_re, c_im)
    d_rep = jnp.repeat(d_skip.reshape(n_groups, SSM_GROUP), CHUNK, axis=1)
    yt = _s5_main(u_g, krow, w_s, v_s, ap1, ap2, d_rep.reshape(n_groups, 1, SSM_GROUP * CHUNK),
                  chunks_per_seq)
    ssm = _glu(yt, w_glu.T.astype(BF16), b_glu, ssm_out_g)
    x1, h2 = _outproj(attn.reshape(t, ATTN_WIDTH), ssm, x2, mod3, attn_out_g,
                      w_out.astype(BF16), norm2_g, seq)
    out = _ffn(h2, w_ff1.astype(BF16), w_ff2.astype(BF16), x1, mod3, seq)
    return out.reshape(bsz, seq, d)


def kernel(x, c, w_ada, b_ada, norm1_g, w_in, q_norm_g, k_norm_g, lam_re, lam_im, log_step,
           b_re, b_im, c_re, c_im, d_skip, w_glu, b_glu, attn_out_g, ssm_out_g, w_out,
           norm2_g, w_ff1, w_ff2):
    for l in range(w_ada.shape[0]):
        x = _layer(x, c, w_ada[l], b_ada[l], norm1_g[l], w_in[l], q_norm_g[l], k_norm_g[l],
                   lam_re[l], lam_im[l], log_step[l], b_re[l], b_im[l], c_re[l], c_im[l],
                   d_skip[l], w_glu[l], b_glu[l], attn_out_g[l], ssm_out_g[l], w_out[l],
                   norm2_g[l], w_ff1[l], w_ff2[l])
    return x
```

```python
import functools
import math

import jax
import jax.numpy as jnp
from jax import lax
from jax.experimental import pallas as pl
from jax.experimental.pallas import tpu as pltpu

F32 = jnp.float32
BF16 = jnp.bfloat16

HEAD_DIM = 128
N_HEADS = 8
ATTN_WIDTH = N_HEADS * HEAD_DIM
SSM_GROUP = 16
STATE_DIM = 64
CHUNK = 128
BAND = 128
DILATIONS = (1, 4, 16)
N_MOD = 6
EPS = 1e-6
NEG = -1e30
LOG2E = math.log2(math.e)

V7X_VMEM_BYTES = 64 * 1024 * 1024
VMEM_LIMIT = V7X_VMEM_BYTES - 8 * 1024 * 1024


def _params(n_axes):
    return pltpu.CompilerParams(
        dimension_semantics=("arbitrary",) * n_axes, vmem_limit_bytes=VMEM_LIMIT)


def _rms(x, eps=EPS):
    return x * lax.rsqrt(jnp.mean(x * x, axis=-1, keepdims=True) + eps)


def _adaln_kernel(c_ref, w_ref, b_ref, o_ref):
    c = c_ref[...]
    s = (c * jax.nn.sigmoid(c)).astype(BF16)
    o_ref[...] = jnp.dot(s, w_ref[...].astype(BF16), preferred_element_type=F32) + b_ref[...]


def _adaln(c, w_ada, b_ada):
    bsz, d = c.shape
    n_out = w_ada.shape[1]
    rows = 8
    tn = 1024
    c8 = jnp.zeros((rows, d), F32).at[:bsz].set(c)
    out = pl.pallas_call(
        _adaln_kernel,
        grid=(n_out // tn,),
        in_specs=[pl.BlockSpec((rows, d), lambda n: (0, 0)),
                  pl.BlockSpec((d, tn), lambda n: (0, n)),
                  pl.BlockSpec((1, tn), lambda n: (0, n))],
        out_specs=pl.BlockSpec((rows, tn), lambda n: (0, n)),
        out_shape=jax.ShapeDtypeStruct((rows, n_out), F32),
        compiler_params=_params(1),
        name="adaln",
    )(c8, w_ada, b_ada.reshape(1, n_out))
    return out[:bsz]


STAGE_PAD = 4


def _inproj_kernel(x_ref, mod_ref, g1_ref, w_ref, wut_ref, qg_ref, kg_ref, qkv_ref, u_ref,
                   h_ref, ustage):
    n = pl.program_id(1)
    tm = h_ref.shape[0]

    @pl.when(n == 0)
    def _():
        shift = mod_ref[0, 0:1, :]
        scale = mod_ref[0, 1:2, :]
        y = _rms(x_ref[...]) * g1_ref[...]
        h_ref[...] = (y * (1.0 + scale) + shift).astype(BF16)

    def head_norm(gain):
        for pair in range(N_HEADS // 2):
            c0 = pair * 2 * HEAD_DIM
            acc = jnp.dot(h_ref[...], w_ref[:, c0:c0 + 2 * HEAD_DIM], preferred_element_type=F32)
            for hd in range(2):
                sl = slice(hd * HEAD_DIM, (hd + 1) * HEAD_DIM)
                qkv_ref[:, c0 + hd * HEAD_DIM:c0 + (hd + 1) * HEAD_DIM] = (
                    _rms(acc[:, sl]) * gain).astype(BF16)

    @pl.when(n == 0)
    def _():
        head_norm(qg_ref[...] * (HEAD_DIM ** -0.5 * LOG2E))

    @pl.when(n == 1)
    def _():
        head_norm(kg_ref[...])

    @pl.when(n == 2)
    def _():
        qkv_ref[...] = jnp.dot(h_ref[...], w_ref[...], preferred_element_type=F32).astype(BF16)

    @pl.when(n == 3)
    def _():
        width = wut_ref.shape[0]
        pitch = width + STAGE_PAD
        sub = 2 * CHUNK
        for c2 in range(tm // sub):
            ut = lax.dot_general(wut_ref[...], h_ref[c2 * sub:(c2 + 1) * sub, :],
                                 (((1,), (1,)), ((), ())), preferred_element_type=F32)
            for cc in range(sub // CHUNK):
                c = c2 * (sub // CHUNK) + cc
                ustage[c * pitch:c * pitch + width, :] = ut[:, cc * CHUNK:(cc + 1) * CHUNK]

        def regroup(g, carry):
            for j in range(SSM_GROUP):
                u_ref[g, :, j * CHUNK:(j + 1) * CHUNK] = (
                    ustage[pl.ds(g * SSM_GROUP + j, tm // CHUNK, stride=pitch), :])
            return carry
        lax.fori_loop(0, width // SSM_GROUP, regroup, 0)


def _inproj(x2, mod3, norm1_g, w_qkv_bf, w_ut_bf, q_norm_g, k_norm_g, seq):
    t, d = x2.shape
    tm = 1024
    tn = ATTN_WIDTH
    width = w_ut_bf.shape[0]
    n_groups = width // SSM_GROUP
    assert w_qkv_bf.shape[1] == 3 * tn and seq % tm == 0 and tm // CHUNK == 8
    per_batch = seq // tm
    return pl.pallas_call(
        _inproj_kernel,
        grid=(t // tm, 4),
        in_specs=[pl.BlockSpec((tm, d), lambda m, n: (m, 0)),
                  pl.BlockSpec((1, N_MOD, d), lambda m, n: (m // per_batch, 0, 0)),
                  pl.BlockSpec((1, d), lambda m, n: (0, 0)),
                  pl.BlockSpec((d, tn), lambda m, n: (0, jnp.minimum(n, 2))),
                  pl.BlockSpec((width, d), lambda m, n: (0, 0), pipeline_mode=pl.Buffered(1)),
                  pl.BlockSpec((1, HEAD_DIM), lambda m, n: (0, 0)),
                  pl.BlockSpec((1, HEAD_DIM), lambda m, n: (0, 0))],
        out_specs=[pl.BlockSpec((tm, tn), lambda m, n: (m, jnp.minimum(n, 2))),
                   pl.BlockSpec((n_groups, tm // CHUNK, SSM_GROUP * CHUNK), lambda m, n: (0, m, 0))],
        out_shape=[jax.ShapeDtypeStruct((t, 3 * tn), BF16),
                   jax.ShapeDtypeStruct((n_groups, t // CHUNK, SSM_GROUP * CHUNK), F32)],
        scratch_shapes=[pltpu.VMEM((tm, d), BF16),
                        pltpu.VMEM((tm // CHUNK * (width + STAGE_PAD), CHUNK), F32)],
        compiler_params=_params(2),
        name="inproj",
    )(x2, mod3, norm1_g.reshape(1, d), w_qkv_bf, w_ut_bf, q_norm_g.reshape(1, HEAD_DIM),
      k_norm_g.reshape(1, HEAD_DIM))


ATTN_SUPER = BAND * max(DILATIONS)
BLOCKS_PER_SUPER = ATTN_SUPER // BAND
WIDE_DILATION = 16
WIDE_PITCH = BAND + 8


def _attn_kernel(slope_ref, q_ref, k_ref, v_ref, o_ref,
                 stage, stage4, q4, q16, k1, k4, k16, v1, v4, v16,
                 bias_ref, oacc, mstat, lstat, *, seq):
    assert DILATIONS == (1, 4, 16)
    n_pat = len(DILATIONS)
    n4 = ATTN_SUPER // 4
    n16 = ATTN_SUPER // 16

    def deinterleave(src, a1, a4, a16, pad):
        def chunk(c, carry):
            r0 = pl.multiple_of(c * ATTN_SUPER, ATTN_SUPER)
            x = src[pl.ds(r0, ATTN_SUPER), :]
            if a1 is not None:
                a1[0, pl.ds(pad + r0, ATTN_SUPER), :] = x
            stage[...] = x.astype(F32)
            for r in range(4):
                y = stage[pl.ds(r, n4, stride=4), :]
                stage4[r * n4:(r + 1) * n4, :] = y
                a4[r, pl.ds(pad + pl.multiple_of(c * n4, n4), n4), :] = y.astype(BF16)
            for r in range(4):
                for a in range(4):
                    z = stage4[pl.ds(r * n4 + a, n16, stride=4), :]
                    a16[4 * a + r, pl.ds(pad + pl.multiple_of(c * n16, n16), n16), :] = (
                        z.astype(BF16))
            return carry
        lax.fori_loop(0, seq // ATTN_SUPER, chunk, 0)

    for buf in (k1, k4, k16, v1, v4, v16):
        buf[:, 0:BAND, :] = jnp.zeros((buf.shape[0], BAND, HEAD_DIM), BF16)
    deinterleave(q_ref, None, q4, q16, 0)
    deinterleave(k_ref, k1, k4, k16, BAND)
    deinterleave(v_ref, v1, v4, v16, BAND)

    qi = lax.broadcasted_iota(jnp.int32, (BAND, 2 * BAND), 0)
    ki = lax.broadcasted_iota(jnp.int32, (BAND, 2 * BAND), 1)
    steps = qi - ki + BAND
    valid = (steps >= 0) & (steps <= BAND)
    slope = slope_ref[...]
    for p, d in enumerate(DILATIONS):
        bias = jnp.where(valid, -(slope * (float(d) * LOG2E)) * steps.astype(F32), NEG)
        bias_ref[2 * p + 1] = bias
        bias_ref[2 * p] = jnp.where(ki >= BAND, bias, NEG)

    q_streams = (None, q4, q16)
    k_streams = (k1, k4, k16)
    v_streams = (v1, v4, v16)
    ones = jnp.ones((2 * BAND, HEAD_DIM), BF16)

    def rows(start, size, d):
        return pl.ds(start, size) if d == 1 else pl.ds(start, size, stride=d)

    def block(p, d, r, j, local):
        row0 = pl.multiple_of(j * BAND, BAND)
        if d == 1:
            q = q_ref[pl.ds(row0, BAND), :]
        else:
            q = q_streams[p][r, pl.ds(row0, BAND), :]
        k2 = k_streams[p][r, pl.ds(row0, 2 * BAND), :]
        v2 = v_streams[p][r, pl.ds(row0, 2 * BAND), :]
        s = lax.dot_general(q, k2, (((1,), (1,)), ((), ())), preferred_element_type=F32)
        has_prev = jnp.minimum(j, 1)
        s = s + bias_ref[2 * p + has_prev]
        m = jnp.max(s, axis=-1, keepdims=True)
        e = jnp.exp2(s - m).astype(BF16)
        acc = jnp.dot(e, jnp.concatenate([v2, ones], axis=1), preferred_element_type=F32)
        if d == WIDE_DILATION:
            dst = pl.ds(pl.multiple_of(r * WIDE_PITCH, 8), BAND)
        else:
            dst = rows(local, BAND, d)
        oacc[p, dst, :] = acc[:, :HEAD_DIM]
        lstat[p, dst, :] = acc[:, HEAD_DIM:]
        mstat[p, dst, :] = jnp.broadcast_to(m, (BAND, HEAD_DIM))

    merge_rows = 256

    def super_block(sb, carry):
        p0 = sb * ATTN_SUPER

        def blocks(idx, c2):
            for p, d in enumerate(DILATIONS):
                r = idx & (d - 1)
                jl = idx >> int(math.log2(d))
                block(p, d, r, sb * (BLOCKS_PER_SUPER // d) + jl, r + BAND * d * jl)
            return c2
        lax.fori_loop(0, BLOCKS_PER_SUPER, blocks, 0, unroll=True)

        def merge(c, c2):
            r = pl.multiple_of(c * merge_rows, merge_rows)

            def natural(buf, p):
                if DILATIONS[p] != WIDE_DILATION:
                    return buf[p, pl.ds(r, merge_rows), :]
                i0 = c * (merge_rows // WIDE_DILATION)
                return jnp.concatenate(
                    [buf[p, pl.ds(i0 + i + half * 8 * WIDE_PITCH, 8, stride=WIDE_PITCH), :]
                     for i in range(merge_rows // WIDE_DILATION)
                     for half in range(WIDE_DILATION // 8)], axis=0)

            ms = [natural(mstat, p) for p in range(n_pat)]
            mx = functools.reduce(jnp.maximum, ms)
            num = jnp.zeros((merge_rows, HEAD_DIM), F32)
            den = jnp.zeros((merge_rows, HEAD_DIM), F32)
            for p in range(n_pat):
                w = jnp.exp2(ms[p] - mx)
                num = num + w * natural(oacc, p)
                den = den + w * natural(lstat, p)
            o_ref[pl.ds(pl.multiple_of(p0 + r, merge_rows), merge_rows), :] = (num / den).astype(BF16)
            return c2
        lax.fori_loop(0, ATTN_SUPER // merge_rows, merge, 0)
        return carry
    lax.fori_loop(0, seq // ATTN_SUPER, super_block, 0)


def _attention(qkv3, slopes):
    bsz, seq, _ = qkv3.shape
    assert seq % ATTN_SUPER == 0
    blk = lambda off: pl.BlockSpec((None, seq, HEAD_DIM), lambda b, h: (b, 0, off + h))
    stream = lambda d, pad: pltpu.VMEM((d, pad + seq // d, HEAD_DIM), BF16)
    stat = pltpu.VMEM((len(DILATIONS), WIDE_DILATION * WIDE_PITCH, HEAD_DIM), F32)
    return pl.pallas_call(
        functools.partial(_attn_kernel, seq=seq),
        grid=(bsz, N_HEADS),
        in_specs=[pl.BlockSpec((None, 1, 2 * BAND), lambda b, h: (h, 0, 0)),
                  blk(0), blk(N_HEADS), blk(2 * N_HEADS)],
        out_specs=pl.BlockSpec((None, seq, HEAD_DIM), lambda b, h: (b, 0, h)),
        out_shape=jax.ShapeDtypeStruct((bsz, seq, ATTN_WIDTH), BF16),
        scratch_shapes=[pltpu.VMEM((ATTN_SUPER, HEAD_DIM), F32),
                        pltpu.VMEM((ATTN_SUPER, HEAD_DIM), F32),
                        stream(4, 0), stream(16, 0),
                        stream(1, BAND), stream(4, BAND), stream(16, BAND),
                        stream(1, BAND), stream(4, BAND), stream(16, BAND),
                        pltpu.VMEM((2 * len(DILATIONS), BAND, 2 * BAND), F32),
                        stat, stat, stat],
        compiler_params=_params(2),
        name="attention",
    )(slopes, qkv3, qkv3, qkv3)


N_DOUBLINGS = 8


def _cmul(ar, ai, br, bi):
    return ar * br - ai * bi, ar * bi + ai * br


def _cpow(ar, ai, e, nbits):
    shape = jnp.broadcast_shapes(ar.shape, e.shape)
    pr = jnp.ones(shape, F32)
    pi = jnp.zeros(shape, F32)
    br, bi = ar, ai
    for k in range(nbits):
        nr, ni = _cmul(pr, pi, br, bi)
        sel = ((e >> k) & 1) == 1
        pr = jnp.where(sel, nr, pr)
        pi = jnp.where(sel, ni, pi)
        br, bi = _cmul(br, bi, br, bi)
    return pr, pi


def _zoh(lr, li, log_step):
    dt = jnp.exp(log_step)
    mag = jnp.exp(lr * dt)
    ar = mag * jnp.cos(li * dt)
    ai = mag * jnp.sin(li * dt)
    den = lr * lr + li * li
    cr = ((ar - 1.0) * lr + ai * li) / den
    ci = (ai * lr - (ar - 1.0) * li) / den
    return ar, ai, cr, ci


def _s5_params_kernel(lr_row, li_row, ls_row, lr_col, li_col, ls_col,
                      bt_re, bt_im, c_re, c_im, ct_re, ct_im,
                      krow_ref, w_ref, v_ref, ap1_ref, ap2_ref):
    p_dim = STATE_DIM
    ar, ai, cr, ci = _zoh(lr_row[...], li_row[...], ls_row[...])
    bbr, bbi = _cmul(cr, ci, bt_re[...], bt_im[...])
    s_idx = lax.broadcasted_iota(jnp.int32, (CHUNK, p_dim), 0)
    qr, qi = _cpow(ar, ai, CHUNK - 1 - s_idx, 7)
    for j in range(SSM_GROUP):
        wr, wi = _cmul(qr, qi, bbr[j:j + 1, :], bbi[j:j + 1, :])
        w_ref[j * CHUNK:(j + 1) * CHUNK, :] = jnp.concatenate([wr, wi], axis=-1).astype(BF16)
    mr, mi = ar, ai
    for _ in range(7):
        mr, mi = _cmul(mr, mi, mr, mi)
    for k in range(N_DOUBLINGS):
        ap1_ref[k:k + 1, :] = jnp.concatenate([mr, mr], axis=-1)
        ap2_ref[k:k + 1, :] = jnp.concatenate([-mi, mi], axis=-1)
        mr, mi = _cmul(mr, mi, mr, mi)

    acr, aci, _, _ = _zoh(lr_col[...], li_col[...], ls_col[...])
    t_idx = lax.broadcasted_iota(jnp.int32, (p_dim, CHUNK), 1)
    pr, pi = _cpow(acr, aci, t_idx, 7)
    cbr, cbi = [], []
    for j in range(SSM_GROUP):
        r_, i_ = _cmul(c_re[...], c_im[...], bbr[j:j + 1, :], bbi[j:j + 1, :])
        cbr.append(r_)
        cbi.append(i_)
    cbr = jnp.concatenate(cbr, axis=0)
    cbi = jnp.concatenate(cbi, axis=0)
    hi = lax.Precision.HIGHEST
    krow_ref[...] = (jnp.dot(cbr, pr, precision=hi, preferred_element_type=F32)
                     - jnp.dot(cbi, pi, precision=hi, preferred_element_type=F32))
    p1r, p1i = _cmul(pr, pi, acr, aci)
    ctr = ct_re[...]
    cti = ct_im[...]
    for i in range(SSM_GROUP):
        vr, vi = _cmul(ctr[:, i:i + 1], cti[:, i:i + 1], p1r, p1i)
        v_ref[0:p_dim, i * CHUNK:(i + 1) * CHUNK] = vr.astype(BF16)
        v_ref[p_dim:2 * p_dim, i * CHUNK:(i + 1) * CHUNK] = (-vi).astype(BF16)


def _s5_params(lam_re, lam_im, log_step, b_re, b_im, c_re, c_im):
    g, p = lam_re.shape
    n = SSM_GROUP
    row = lambda a: a.reshape(g, 1, p)
    col = lambda a: a.reshape(g, p, 1)
    ls = jnp.broadcast_to(log_step[:, None], (g, p))
    tr = lambda a: jnp.swapaxes(a, 1, 2)
    spec = lambda *shape: pl.BlockSpec((None,) + shape, lambda i: (i,) + (0,) * len(shape))
    return pl.pallas_call(
        _s5_params_kernel,
        grid=(g,),
        in_specs=[spec(1, p)] * 3 + [spec(p, 1)] * 3 + [spec(n, p)] * 4 + [spec(p, n)] * 2,
        out_specs=[spec(n * n, CHUNK), spec(n * CHUNK, 2 * p), spec(2 * p, n * CHUNK),
                   spec(N_DOUBLINGS, 2 * p), spec(N_DOUBLINGS, 2 * p)],
        out_shape=[jax.ShapeDtypeStruct((g, n * n, CHUNK), F32),
                   jax.ShapeDtypeStruct((g, n * CHUNK, 2 * p), BF16),
                   jax.ShapeDtypeStruct((g, 2 * p, n * CHUNK), BF16),
                   jax.ShapeDtypeStruct((g, N_DOUBLINGS, 2 * p), F32),
                   jax.ShapeDtypeStruct((g, N_DOUBLINGS, 2 * p), F32)],
        compiler_params=_params(1),
        name="s5_params",
    )(row(lam_re), row(lam_im), row(ls), col(lam_re), col(lam_im), col(ls),
      tr(b_re), tr(b_im), c_re, c_im, tr(c_re), tr(c_im))


GROUPS_PER_STEP = 2
T_COLS = 4 * CHUNK


def _s5_kernel(u_ref, krow_ref, knext_ref, w_ref, v_ref, ap1_ref, ap2_ref, d_ref, yt_ref,
               t_ref, ystage, *, chunks_per_seq):
    n = SSM_GROUP
    n_chunks = u_ref.shape[1]
    n_slabs = n * CHUNK // T_COLS
    per_slab = T_COLS // CHUNK

    s_idx = lax.broadcasted_iota(jnp.int32, (CHUNK, CHUNK), 0)
    t_idx = lax.broadcasted_iota(jnp.int32, (CHUNK, CHUNK), 1)
    causal = t_idx >= s_idx
    pos = lax.broadcasted_iota(jnp.int32, (n_chunks, 2 * STATE_DIM), 0) & (chunks_per_seq - 1)

    def build_slab(taps_ref, slot, c):
        def rows_j(j, carry):
            for e in range(per_slab):
                row = j * n + c * per_slab + e
                taps = jnp.broadcast_to(taps_ref[pl.ds(row, 1), :], (CHUNK, CHUNK))
                shifted = pltpu.roll(taps, 0, 1, stride=1, stride_axis=0)
                t_ref[slot, c, pl.ds(pl.multiple_of(j * CHUNK, CHUNK), CHUNK),
                      e * CHUNK:(e + 1) * CHUNK] = jnp.where(causal, shifted, 0.0).astype(BF16)
            return carry
        lax.fori_loop(0, n, rows_j, 0, unroll=True)

    def shift_rows(x, k):
        return jnp.where(pos >= k, pltpu.roll(x, k, 0), 0.0)

    def group(q, next_taps_ref):
        u = u_ref[q]
        ub = u.astype(BF16)
        b = jnp.dot(ub, w_ref[q], preferred_element_type=F32)
        h = shift_rows(b, 1)
        for k in range(int(math.log2(chunks_per_seq))):
            hs = shift_rows(h, 1 << k)
            h = (h + hs * ap1_ref[q, k:k + 1, :]
                 + pltpu.roll(hs, STATE_DIM, 1) * ap2_ref[q, k:k + 1, :])
        hv = jnp.dot(h.astype(BF16), v_ref[q], preferred_element_type=F32)

        def rebuild(c, carry):
            build_slab(next_taps_ref, 1 - q, c)
            return carry
        lax.fori_loop(0, n_slabs, rebuild, 0, unroll=True)

        pitch = n_chunks + STAGE_PAD
        base = q * n * pitch
        for c in range(n_slabs):
            yc = jnp.dot(ub, t_ref[q, c], preferred_element_type=F32)
            for e in range(per_slab):
                i = c * per_slab + e
                cols = slice(i * CHUNK, (i + 1) * CHUNK)
                yi = yc[:, e * CHUNK:(e + 1) * CHUNK] + hv[:, cols] + d_ref[q, :, cols] * u[:, cols]
                ystage[base + i * pitch:base + i * pitch + n_chunks, :] = jax.nn.gelu(yi)
        for c in range(n_chunks):
            for i0 in range(0, n, 8):
                yt_ref[q * n + i0:q * n + i0 + 8, c * CHUNK:(c + 1) * CHUNK] = (
                    ystage[pl.ds(base + i0 * pitch + c, 8, stride=pitch), :])

    @pl.when(pl.program_id(0) == 0)
    def _():
        def first(c, carry):
            build_slab(krow_ref.at[0], 0, c)
            return carry
        lax.fori_loop(0, n_slabs, first, 0)

    group(0, krow_ref.at[1])
    group(1, knext_ref)


def _s5_main(u_g, krow, w, v, ap1, ap2, d_rep, chunks_per_seq):
    g, n_chunks, width = u_g.shape
    gs = GROUPS_PER_STEP
    assert chunks_per_seq & (chunks_per_seq - 1) == 0 and chunks_per_seq <= 1 << N_DOUBLINGS
    assert g % gs == 0 and gs == 2
    spec = lambda *shape: pl.BlockSpec((gs,) + shape, lambda i: (i,) + (0,) * len(shape))
    taps = (SSM_GROUP * SSM_GROUP, CHUNK)
    return pl.pallas_call(
        functools.partial(_s5_kernel, chunks_per_seq=chunks_per_seq),
        grid=(g // gs,),
        in_specs=[spec(n_chunks, width), spec(*taps),
                  pl.BlockSpec((None,) + taps, lambda i: (jnp.minimum(gs * i + gs, g - 1), 0, 0)),
                  spec(width, 2 * STATE_DIM), spec(2 * STATE_DIM, width),
                  spec(N_DOUBLINGS, 2 * STATE_DIM), spec(N_DOUBLINGS, 2 * STATE_DIM),
                  spec(1, width)],
        out_specs=pl.BlockSpec((gs * SSM_GROUP, n_chunks * CHUNK), lambda i: (i, 0)),
        out_shape=jax.ShapeDtypeStruct((g * SSM_GROUP, n_chunks * CHUNK), F32),
        scratch_shapes=[pltpu.VMEM((gs, width // T_COLS, width, T_COLS), BF16),
                        pltpu.VMEM((gs * SSM_GROUP * (n_chunks + STAGE_PAD), CHUNK), F32)],
        compiler_params=_params(1),
        name="s5_main",
    )(u_g, krow, krow, w, v, ap1, ap2, d_rep)


def _glu_kernel(yt_ref, wt_ref, b_ref, g_ref, o_ref):
    y = yt_ref[...]
    z = jnp.dot(wt_ref[...], y.astype(BF16), preferred_element_type=F32) + b_ref[...]
    ssm = y * jax.nn.sigmoid(z)
    inv = lax.rsqrt(jnp.mean(ssm * ssm, axis=0, keepdims=True) + EPS)
    o_ref[...] = (ssm * inv * g_ref[...]).astype(BF16)


def _glu(yt, w_glu_t_bf, b_glu, ssm_out_g):
    w, t = yt.shape
    tn = 1024
    return pl.pallas_call(
        _glu_kernel,
        grid=(t // tn,),
        in_specs=[pl.BlockSpec((w, tn), lambda m: (0, m)),
                  pl.BlockSpec((w, w), lambda m: (0, 0)),
                  pl.BlockSpec((w, 1), lambda m: (0, 0)),
                  pl.BlockSpec((w, 1), lambda m: (0, 0))],
        out_specs=pl.BlockSpec((w, tn), lambda m: (0, m)),
        out_shape=jax.ShapeDtypeStruct((w, t), BF16),
        compiler_params=_params(1),
        name="glu",
    )(yt, w_glu_t_bf, b_glu.reshape(w, 1), ssm_out_g.reshape(w, 1))


OUT_COLS = 512


def _outproj_kernel(attn_ref, ssmt_ref, x_ref, mod_ref, ag_ref, wa_ref, ws_ref, g2_ref,
                    x1_ref, h2_ref):
    d = x_ref.shape[1]
    attn_n = (_rms(attn_ref[...].astype(F32)) * ag_ref[...]).astype(BF16)
    ssmt = ssmt_ref[...]
    sq = jnp.zeros((x_ref.shape[0], 1), F32)
    for c0 in range(0, d, OUT_COLS):
        cols = slice(c0, c0 + OUT_COLS)
        mixed = jnp.dot(attn_n, wa_ref[:, cols], preferred_element_type=F32)
        mixed = mixed + lax.dot_general(ssmt, ws_ref[:, cols], (((0,), (0,)), ((), ())),
                                        preferred_element_type=F32)
        x1 = x_ref[:, cols] + mod_ref[0, 2:3, cols] * mixed
        x1_ref[:, cols] = x1
        sq = sq + jnp.sum(x1 * x1, axis=-1, keepdims=True)
    inv = lax.rsqrt(sq * (1.0 / d) + EPS)
    for c0 in range(0, d, OUT_COLS):
        cols = slice(c0, c0 + OUT_COLS)
        gain = g2_ref[:, cols] * (1.0 + mod_ref[0, 4:5, cols])
        h2_ref[:, cols] = (x1_ref[:, cols] * inv * gain + mod_ref[0, 3:4, cols]).astype(BF16)


def _outproj(attn2, ssmt, x2, mod3, attn_out_g, w_out_bf, norm2_g, seq):
    t, d = x2.shape
    wa = attn2.shape[1]
    ws = ssmt.shape[0]
    tm = 512
    per_batch = seq // tm
    return pl.pallas_call(
        _outproj_kernel,
        grid=(t // tm,),
        in_specs=[pl.BlockSpec((tm, wa), lambda m: (m, 0)),
                  pl.BlockSpec((ws, tm), lambda m: (0, m)),
                  pl.BlockSpec((tm, d), lambda m: (m, 0)),
                  pl.BlockSpec((1, N_MOD, d), lambda m: (m // per_batch, 0, 0)),
                  pl.BlockSpec((1, wa), lambda m: (0, 0)),
                  pl.BlockSpec((wa, d), lambda m: (0, 0)),
                  pl.BlockSpec((ws, d), lambda m: (1, 0)),
                  pl.BlockSpec((1, d), lambda m: (0, 0))],
        out_specs=[pl.BlockSpec((tm, d), lambda m: (m, 0)),
                   pl.BlockSpec((tm, d), lambda m: (m, 0))],
        out_shape=[jax.ShapeDtypeStruct((t, d), F32),
                   jax.ShapeDtypeStruct((t, d), BF16)],
        compiler_params=_params(1),
        name="outproj",
    )(attn2, ssmt, x2, mod3, attn_out_g.reshape(1, wa), w_out_bf, w_out_bf,
      norm2_g.reshape(1, d))


def _ffn_kernel(h_ref, w1_ref, w2_ref, x1_ref, mod_ref, o_ref):
    f = pl.program_id(1)

    def partial_sum():
        a = jnp.dot(h_ref[...], w1_ref[...], preferred_element_type=F32)
        a = jnp.square(jnp.maximum(a, 0.0)).astype(BF16)
        return jnp.dot(a, w2_ref[...], preferred_element_type=F32)

    @pl.when(f == 0)
    def _():
        o_ref[...] = partial_sum()

    @pl.when(f > 0)
    def _():
        o_ref[...] += partial_sum()

    @pl.when(f == pl.num_programs(1) - 1)
    def _():
        o_ref[...] = x1_ref[...] + mod_ref[0, 5:6, :] * o_ref[...]


def _ffn(h2, w1_bf, w2_bf, x1, mod3, seq):
    t, d = x1.shape
    dff = w1_bf.shape[1]
    tm = 512
    tf = 1024
    per_batch = seq // tm
    return pl.pallas_call(
        _ffn_kernel,
        grid=(t // tm, dff // tf),
        in_specs=[pl.BlockSpec((tm, d), lambda m, f: (m, 0)),
                  pl.BlockSpec((d, tf), lambda m, f: (0, f)),
                  pl.BlockSpec((tf, d), lambda m, f: (f, 0)),
                  pl.BlockSpec((tm, d), lambda m, f: (m, 0)),
                  pl.BlockSpec((1, N_MOD, d), lambda m, f: (m // per_batch, 0, 0))],
        out_specs=pl.BlockSpec((tm, d), lambda m, f: (m, 0)),
        out_shape=jax.ShapeDtypeStruct((t, d), F32),
        compiler_params=_params(2),
        name="ffn",
    )(h2, w1_bf, w2_bf, x1, mod3)


def _layer(x, c, w_ada, b_ada, norm1_g, w_in, q_norm_g, k_norm_g, lam_re, lam_im, log_step,
           b_re, b_im, c_re, c_im, d_skip, w_glu, b_glu, attn_out_g, ssm_out_g, w_out,
           norm2_g, w_ff1, w_ff2):
    bsz, seq, d = x.shape
    t = bsz * seq
    n_groups = lam_re.shape[0]
    chunks_per_seq = seq // CHUNK

    mod3 = _adaln(c, w_ada, b_ada).reshape(bsz, N_MOD, d)
    x2 = x.reshape(t, d)
    w_in_bf = w_in.astype(BF16)
    qkv, u_g = _inproj(x2, mod3, norm1_g, w_in_bf[:, :3 * ATTN_WIDTH], w_in_bf[:, 3 * ATTN_WIDTH:].T,
                       q_norm_g, k_norm_g, seq)

    slopes = 2.0 ** (-8.0 * (jnp.arange(N_HEADS, dtype=F32) + 1.0) / N_HEADS)
    slopes = jnp.broadcast_to(slopes[:, None, None], (N_HEADS, 1, 2 * BAND))
    attn = _attention(qkv.reshape(bsz, seq, 3 * ATTN_WIDTH), slopes)

    krow, w_s, v_s, ap1, ap2 = _s5_params(lam_re, lam_im, log_step, b_re, b_im, c_re, c_im)
    d_rep = jnp.repeat(d_skip.reshape(n_groups, SSM_GROUP), CHUNK, axis=1)
    yt = _s5_main(u_g, krow, w_s, v_s, ap1, ap2, d_rep.reshape(n_groups, 1, SSM_GROUP * CHUNK),
                  chunks_per_seq)
    ssm = _glu(yt, w_glu.T.astype(BF16), b_glu, ssm_out_g)
    x1, h2 = _outproj(attn.reshape(t, ATTN_WIDTH), ssm, x2, mod3, attn_out_g,
                      w_out.astype(BF16), norm2_g, seq)
    out = _ffn(h2, w_ff1.astype(BF16), w_ff2.astype(BF16), x1, mod3, seq)
    return out.reshape(bsz, seq, d)


def kernel(x, c, w_ada, b_ada, norm1_g, w_in, q_norm_g, k_norm_g, lam_re, lam_im, log_step,
           b_re, b_im, c_re, c_im, d_skip, w_glu, b_glu, attn_out_g, ssm_out_g, w_out,
           norm2_g, w_ff1, w_ff2):
    for l in range(w_ada.shape[0]):
        x = _layer(x, c, w_ada[l], b_ada[l], norm1_g[l], w_in[l], q_norm_g[l], k_norm_g[l],
                   lam_re[l], lam_im[l], log_step[l], b_re[l], b_im[l], c_re[l], c_im[l],
                   d_skip[l], w_glu[l], b_glu[l], attn_out_g[l], ssm_out_g[l], w_out[l],
                   norm2_g[l], w_ff1[l], w_ff2[l])
    return x
```

```python
import functools
import math

import jax
import jax.numpy as jnp
from jax import lax
from jax.experimental import pallas as pl
from jax.experimental.pallas import tpu as pltpu

F32 = jnp.float32
BF16 = jnp.bfloat16

HEAD_DIM = 128
N_HEADS = 8
ATTN_WIDTH = N_HEADS * HEAD_DIM
SSM_GROUP = 16
STATE_DIM = 64
CHUNK = 128
BAND = 128
DILATIONS = (1, 4, 16)
N_MOD = 6
EPS = 1e-6
NEG = -1e30
LOG2E = math.log2(math.e)

V7X_VMEM_BYTES = 64 * 1024 * 1024
V7X_SUBLANES = 8
VMEM_LIMIT = V7X_VMEM_BYTES - 8 * 1024 * 1024

ADALN_COLS = 1024
INPROJ_ROWS = 1024
GLU_TOKENS = 1024
OUTPROJ_ROWS = 512
FFN_ROWS = 512
FFN_HIDDEN = 1024


def _params(n_axes):
    return pltpu.CompilerParams(
        dimension_semantics=("arbitrary",) * n_axes, vmem_limit_bytes=VMEM_LIMIT)


def _rms(x, eps=EPS):
    return x * lax.rsqrt(jnp.mean(x * x, axis=-1, keepdims=True) + eps)


def _adaln_kernel(c_ref, w_ref, b_ref, o_ref):
    c = c_ref[...]
    s = (c * jax.nn.sigmoid(c)).astype(BF16)
    o_ref[...] = jnp.dot(s, w_ref[...].astype(BF16), preferred_element_type=F32) + b_ref[...]


def _adaln(c, w_ada, b_ada):
    bsz, d = c.shape
    n_out = w_ada.shape[1]
    rows = V7X_SUBLANES
    tn = ADALN_COLS
    assert bsz <= rows
    c8 = jnp.zeros((rows, d), F32).at[:bsz].set(c)
    out = pl.pallas_call(
        _adaln_kernel,
        grid=(n_out // tn,),
        in_specs=[pl.BlockSpec((rows, d), lambda n: (0, 0)),
                  pl.BlockSpec((d, tn), lambda n: (0, n)),
                  pl.BlockSpec((1, tn), lambda n: (0, n))],
        out_specs=pl.BlockSpec((rows, tn), lambda n: (0, n)),
        out_shape=jax.ShapeDtypeStruct((rows, n_out), F32),
        compiler_params=_params(1),
        name="adaln",
    )(c8, w_ada, b_ada.reshape(1, n_out))
    return out[:bsz]


STAGE_PAD = 4


def _inproj_kernel(x_ref, mod_ref, g1_ref, w_ref, wut_ref, qg_ref, kg_ref, qkv_ref, u_ref,
                   h_ref, ustage):
    n = pl.program_id(1)
    tm = h_ref.shape[0]

    @pl.when(n == 0)
    def _():
        shift = mod_ref[0, 0:1, :]
        scale = mod_ref[0, 1:2, :]
        y = _rms(x_ref[...]) * g1_ref[...]
        h_ref[...] = (y * (1.0 + scale) + shift).astype(BF16)

    def head_norm(gain):
        for pair in range(N_HEADS // 2):
            c0 = pair * 2 * HEAD_DIM
            acc = jnp.dot(h_ref[...], w_ref[:, c0:c0 + 2 * HEAD_DIM], preferred_element_type=F32)
            for hd in range(2):
                sl = slice(hd * HEAD_DIM, (hd + 1) * HEAD_DIM)
                qkv_ref[:, c0 + hd * HEAD_DIM:c0 + (hd + 1) * HEAD_DIM] = (
                    _rms(acc[:, sl]) * gain).astype(BF16)

    @pl.when(n == 0)
    def _():
        head_norm(qg_ref[...] * (HEAD_DIM ** -0.5 * LOG2E))

    @pl.when(n == 1)
    def _():
        head_norm(kg_ref[...])

    @pl.when(n == 2)
    def _():
        qkv_ref[...] = jnp.dot(h_ref[...], w_ref[...], preferred_element_type=F32).astype(BF16)

    @pl.when(n == 3)
    def _():
        width = wut_ref.shape[0]
        pitch = width + STAGE_PAD
        sub = 2 * CHUNK
        for c2 in range(tm // sub):
            ut = lax.dot_general(wut_ref[...], h_ref[c2 * sub:(c2 + 1) * sub, :],
                                 (((1,), (1,)), ((), ())), preferred_element_type=F32)
            for cc in range(sub // CHUNK):
                c = c2 * (sub // CHUNK) + cc
                ustage[c * pitch:c * pitch + width, :] = ut[:, cc * CHUNK:(cc + 1) * CHUNK]

        def regroup(g, carry):
            for j in range(SSM_GROUP):
                u_ref[g, :, j * CHUNK:(j + 1) * CHUNK] = (
                    ustage[pl.ds(g * SSM_GROUP + j, tm // CHUNK, stride=pitch), :])
            return carry
        lax.fori_loop(0, width // SSM_GROUP, regroup, 0)


def _inproj(x2, mod3, norm1_g, w_qkv_bf, w_ut_bf, q_norm_g, k_norm_g, seq):
    t, d = x2.shape
    tm = INPROJ_ROWS
    tn = ATTN_WIDTH
    width = w_ut_bf.shape[0]
    n_groups = width // SSM_GROUP
    assert w_qkv_bf.shape[1] == 3 * tn and seq % tm == 0 and tm // CHUNK == V7X_SUBLANES
    per_batch = seq // tm
    return pl.pallas_call(
        _inproj_kernel,
        grid=(t // tm, 4),
        in_specs=[pl.BlockSpec((tm, d), lambda m, n: (m, 0)),
                  pl.BlockSpec((1, N_MOD, d), lambda m, n: (m // per_batch, 0, 0)),
                  pl.BlockSpec((1, d), lambda m, n: (0, 0)),
                  pl.BlockSpec((d, tn), lambda m, n: (0, jnp.minimum(n, 2))),
                  pl.BlockSpec((width, d), lambda m, n: (0, 0), pipeline_mode=pl.Buffered(1)),
                  pl.BlockSpec((1, HEAD_DIM), lambda m, n: (0, 0)),
                  pl.BlockSpec((1, HEAD_DIM), lambda m, n: (0, 0))],
        out_specs=[pl.BlockSpec((tm, tn), lambda m, n: (m, jnp.minimum(n, 2))),
                   pl.BlockSpec((n_groups, tm // CHUNK, SSM_GROUP * CHUNK), lambda m, n: (0, m, 0))],
        out_shape=[jax.ShapeDtypeStruct((t, 3 * tn), BF16),
                   jax.ShapeDtypeStruct((n_groups, t // CHUNK, SSM_GROUP * CHUNK), F32)],
        scratch_shapes=[pltpu.VMEM((tm, d), BF16),
                        pltpu.VMEM((tm // CHUNK * (width + STAGE_PAD), CHUNK), F32)],
        compiler_params=_params(2),
        name="inproj",
    )(x2, mod3, norm1_g.reshape(1, d), w_qkv_bf, w_ut_bf, q_norm_g.reshape(1, HEAD_DIM),
      k_norm_g.reshape(1, HEAD_DIM))


ATTN_SUPER = BAND * max(DILATIONS)
BLOCKS_PER_SUPER = ATTN_SUPER // BAND
WIDE_DILATION = 16
WIDE_PITCH = BAND + 8


def _attn_kernel(slope_ref, q_ref, k_ref, v_ref, o_ref,
                 stage, stage4, q4, q16, k1, k4, k16, v1, v4, v16,
                 bias_ref, oacc, mstat, lstat, *, seq):
    assert DILATIONS == (1, 4, 16)
    n_pat = len(DILATIONS)
    n4 = ATTN_SUPER // 4
    n16 = ATTN_SUPER // 16

    def deinterleave(src, a1, a4, a16, pad):
        def chunk(c, carry):
            r0 = pl.multiple_of(c * ATTN_SUPER, ATTN_SUPER)
            x = src[pl.ds(r0, ATTN_SUPER), :]
            if a1 is not None:
                a1[0, pl.ds(pad + r0, ATTN_SUPER), :] = x
            stage[...] = x.astype(F32)
            for r in range(4):
                y = stage[pl.ds(r, n4, stride=4), :]
                stage4[r * n4:(r + 1) * n4, :] = y
                a4[r, pl.ds(pad + pl.multiple_of(c * n4, n4), n4), :] = y.astype(BF16)
            for r in range(4):
                for a in range(4):
                    z = stage4[pl.ds(r * n4 + a, n16, stride=4), :]
                    a16[4 * a + r, pl.ds(pad + pl.multiple_of(c * n16, n16), n16), :] = (
                        z.astype(BF16))
            return carry
        lax.fori_loop(0, seq // ATTN_SUPER, chunk, 0)

    for buf in (k1, k4, k16, v1, v4, v16):
        buf[:, 0:BAND, :] = jnp.zeros((buf.shape[0], BAND, HEAD_DIM), BF16)
    deinterleave(q_ref, None, q4, q16, 0)
    deinterleave(k_ref, k1, k4, k16, BAND)
    deinterleave(v_ref, v1, v4, v16, BAND)

    qi = lax.broadcasted_iota(jnp.int32, (BAND, 2 * BAND), 0)
    ki = lax.broadcasted_iota(jnp.int32, (BAND, 2 * BAND), 1)
    steps = qi - ki + BAND
    valid = (steps >= 0) & (steps <= BAND)
    slope = slope_ref[...]
    for p, d in enumerate(DILATIONS):
        bias = jnp.where(valid, -(slope * (float(d) * LOG2E)) * steps.astype(F32), NEG)
        bias_ref[2 * p + 1] = bias
        bias_ref[2 * p] = jnp.where(ki >= BAND, bias, NEG)

    q_streams = (None, q4, q16)
    k_streams = (k1, k4, k16)
    v_streams = (v1, v4, v16)
    ones = jnp.ones((2 * BAND, HEAD_DIM), BF16)

    def rows(start, size, d):
        return pl.ds(start, size) if d == 1 else pl.ds(start, size, stride=d)

    def block(p, d, r, j, local):
        row0 = pl.multiple_of(j * BAND, BAND)
        if d == 1:
            q = q_ref[pl.ds(row0, BAND), :]
        else:
            q = q_streams[p][r, pl.ds(row0, BAND), :]
        k2 = k_streams[p][r, pl.ds(row0, 2 * BAND), :]
        v2 = v_streams[p][r, pl.ds(row0, 2 * BAND), :]
        s = lax.dot_general(q, k2, (((1,), (1,)), ((), ())), preferred_element_type=F32)
        has_prev = jnp.minimum(j, 1)
        s = s + bias_ref[2 * p + has_prev]
        m = jnp.max(s, axis=-1, keepdims=True)
        e = jnp.exp2(s - m).astype(BF16)
        acc = jnp.dot(e, jnp.concatenate([v2, ones], axis=1), preferred_element_type=F32)
        if d == WIDE_DILATION:
            dst = pl.ds(pl.multiple_of(r * WIDE_PITCH, 8), BAND)
        else:
            dst = rows(local, BAND, d)
        oacc[p, dst, :] = acc[:, :HEAD_DIM]
        lstat[p, dst, :] = acc[:, HEAD_DIM:]
        mstat[p, dst, :] = jnp.broadcast_to(m, (BAND, HEAD_DIM))

    merge_rows = 256

    def super_block(sb, carry):
        p0 = sb * ATTN_SUPER

        def blocks(idx, c2):
            for p, d in enumerate(DILATIONS):
                r = idx & (d - 1)
                jl = idx >> int(math.log2(d))
                block(p, d, r, sb * (BLOCKS_PER_SUPER // d) + jl, r + BAND * d * jl)
            return c2
        lax.fori_loop(0, BLOCKS_PER_SUPER, blocks, 0, unroll=True)

        def merge(c, c2):
            r = pl.multiple_of(c * merge_rows, merge_rows)

            def natural(buf, p):
                if DILATIONS[p] != WIDE_DILATION:
                    return buf[p, pl.ds(r, merge_rows), :]
                i0 = c * (merge_rows // WIDE_DILATION)
                return jnp.concatenate(
                    [buf[p, pl.ds(i0 + i + half * 8 * WIDE_PITCH, 8, stride=WIDE_PITCH), :]
                     for i in range(merge_rows // WIDE_DILATION)
                     for half in range(WIDE_DILATION // 8)], axis=0)

            ms = [natural(mstat, p) for p in range(n_pat)]
            mx = functools.reduce(jnp.maximum, ms)
            num = jnp.zeros((merge_rows, HEAD_DIM), F32)
            den = jnp.zeros((merge_rows, HEAD_DIM), F32)
            for p in range(n_pat):
                w = jnp.exp2(ms[p] - mx)
                num = num + w * natural(oacc, p)
                den = den + w * natural(lstat, p)
            o_ref[pl.ds(pl.multiple_of(p0 + r, merge_rows), merge_rows), :] = (num / den).astype(BF16)
            return c2
        lax.fori_loop(0, ATTN_SUPER // merge_rows, merge, 0)
        return carry
    lax.fori_loop(0, seq // ATTN_SUPER, super_block, 0)


def _attention(qkv3, slopes):
    bsz, seq, _ = qkv3.shape
    assert seq % ATTN_SUPER == 0
    blk = lambda off: pl.BlockSpec((None, seq, HEAD_DIM), lambda b, h: (b, 0, off + h))
    stream = lambda d, pad: pltpu.VMEM((d, pad + seq // d, HEAD_DIM), BF16)
    stat = pltpu.VMEM((len(DILATIONS), WIDE_DILATION * WIDE_PITCH, HEAD_DIM), F32)
    return pl.pallas_call(
        functools.partial(_attn_kernel, seq=seq),
        grid=(bsz, N_HEADS),
        in_specs=[pl.BlockSpec((None, 1, 2 * BAND), lambda b, h: (h, 0, 0)),
                  blk(0), blk(N_HEADS), blk(2 * N_HEADS)],
        out_specs=pl.BlockSpec((None, seq, HEAD_DIM), lambda b, h: (b, 0, h)),
        out_shape=jax.ShapeDtypeStruct((bsz, seq, ATTN_WIDTH), BF16),
        scratch_shapes=[pltpu.VMEM((ATTN_SUPER, HEAD_DIM), F32),
                        pltpu.VMEM((ATTN_SUPER, HEAD_DIM), F32),
                        stream(4, 0), stream(16, 0),
                        stream(1, BAND), stream(4, BAND), stream(16, BAND),
                        stream(1, BAND), stream(4, BAND), stream(16, BAND),
                        pltpu.VMEM((2 * len(DILATIONS), BAND, 2 * BAND), F32),
                        stat, stat, stat],
        compiler_params=_params(2),
        name="attention",
    )(slopes, qkv3, qkv3, qkv3)


N_DOUBLINGS = 8


def _cmul(ar, ai, br, bi):
    return ar * br - ai * bi, ar * bi + ai * br


def _cpow(ar, ai, e, nbits):
    shape = jnp.broadcast_shapes(ar.shape, e.shape)
    pr = jnp.ones(shape, F32)
    pi = jnp.zeros(shape, F32)
    br, bi = ar, ai
    for k in range(nbits):
        nr, ni = _cmul(pr, pi, br, bi)
        sel = ((e >> k) & 1) == 1
        pr = jnp.where(sel, nr, pr)
        pi = jnp.where(sel, ni, pi)
        br, bi = _cmul(br, bi, br, bi)
    return pr, pi


def _zoh(lr, li, log_step):
    dt = jnp.exp(log_step)
    mag = jnp.exp(lr * dt)
    ar = mag * jnp.cos(li * dt)
    ai = mag * jnp.sin(li * dt)
    den = lr * lr + li * li
    cr = ((ar - 1.0) * lr + ai * li) / den
    ci = (ai * lr - (ar - 1.0) * li) / den
    return ar, ai, cr, ci


def _s5_params_kernel(lr_row, li_row, ls_row, lr_col, li_col, ls_col,
                      bt_re, bt_im, c_re, c_im, ct_re, ct_im,
                      krow_ref, w_ref, v_ref, ap1_ref, ap2_ref):
    p_dim = STATE_DIM
    ar, ai, cr, ci = _zoh(lr_row[...], li_row[...], ls_row[...])
    bbr, bbi = _cmul(cr, ci, bt_re[...], bt_im[...])
    s_idx = lax.broadcasted_iota(jnp.int32, (CHUNK, p_dim), 0)
    qr, qi = _cpow(ar, ai, CHUNK - 1 - s_idx, 7)
    for j in range(SSM_GROUP):
        wr, wi = _cmul(qr, qi, bbr[j:j + 1, :], bbi[j:j + 1, :])
        w_ref[j * CHUNK:(j + 1) * CHUNK, :] = jnp.concatenate([wr, wi], axis=-1).astype(BF16)
    mr, mi = ar, ai
    for _ in range(7):
        mr, mi = _cmul(mr, mi, mr, mi)
    for k in range(N_DOUBLINGS):
        ap1_ref[k:k + 1, :] = jnp.concatenate([mr, mr], axis=-1)
        ap2_ref[k:k + 1, :] = jnp.concatenate([-mi, mi], axis=-1)
        mr, mi = _cmul(mr, mi, mr, mi)

    acr, aci, _, _ = _zoh(lr_col[...], li_col[...], ls_col[...])
    t_idx = lax.broadcasted_iota(jnp.int32, (p_dim, CHUNK), 1)
    pr, pi = _cpow(acr, aci, t_idx, 7)
    cbr, cbi = [], []
    for j in range(SSM_GROUP):
        r_, i_ = _cmul(c_re[...], c_im[...], bbr[j:j + 1, :], bbi[j:j + 1, :])
        cbr.append(r_)
        cbi.append(i_)
    cbr = jnp.concatenate(cbr, axis=0)
    cbi = jnp.concatenate(cbi, axis=0)
    hi = lax.Precision.HIGHEST
    krow_ref[...] = (jnp.dot(cbr, pr, precision=hi, preferred_element_type=F32)
                     - jnp.dot(cbi, pi, precision=hi, preferred_element_type=F32))
    p1r, p1i = _cmul(pr, pi, acr, aci)
    ctr = ct_re[...]
    cti = ct_im[...]
    for i in range(SSM_GROUP):
        vr, vi = _cmul(ctr[:, i:i + 1], cti[:, i:i + 1], p1r, p1i)
        v_ref[0:p_dim, i * CHUNK:(i + 1) * CHUNK] = vr.astype(BF16)
        v_ref[p_dim:2 * p_dim, i * CHUNK:(i + 1) * CHUNK] = (-vi).astype(BF16)


def _s5_params(lam_re, lam_im, log_step, b_re, b_im, c_re, c_im):
    g, p = lam_re.shape
    n = SSM_GROUP
    row = lambda a: a.reshape(g, 1, p)
    col = lambda a: a.reshape(g, p, 1)
    ls = jnp.broadcast_to(log_step[:, None], (g, p))
    tr = lambda a: jnp.swapaxes(a, 1, 2)
    spec = lambda *shape: pl.BlockSpec((None,) + shape, lambda i: (i,) + (0,) * len(shape))
    return pl.pallas_call(
        _s5_params_kernel,
        grid=(g,),
        in_specs=[spec(1, p)] * 3 + [spec(p, 1)] * 3 + [spec(n, p)] * 4 + [spec(p, n)] * 2,
        out_specs=[spec(n * n, CHUNK), spec(n * CHUNK, 2 * p), spec(2 * p, n * CHUNK),
                   spec(N_DOUBLINGS, 2 * p), spec(N_DOUBLINGS, 2 * p)],
        out_shape=[jax.ShapeDtypeStruct((g, n * n, CHUNK), F32),
                   jax.ShapeDtypeStruct((g, n * CHUNK, 2 * p), BF16),
                   jax.ShapeDtypeStruct((g, 2 * p, n * CHUNK), BF16),
                   jax.ShapeDtypeStruct((g, N_DOUBLINGS, 2 * p), F32),
                   jax.ShapeDtypeStruct((g, N_DOUBLINGS, 2 * p), F32)],
        compiler_params=_params(1),
        name="s5_params",
    )(row(lam_re), row(lam_im), row(ls), col(lam_re), col(lam_im), col(ls),
      tr(b_re), tr(b_im), c_re, c_im, tr(c_re), tr(c_im))


GROUPS_PER_STEP = 2
T_COLS = 4 * CHUNK


def _s5_kernel(u_ref, krow_ref, knext_ref, w_ref, v_ref, ap1_ref, ap2_ref, d_ref, yt_ref,
               t_ref, ystage, *, chunks_per_seq):
    n = SSM_GROUP
    n_chunks = u_ref.shape[1]
    n_slabs = n * CHUNK // T_COLS
    per_slab = T_COLS // CHUNK

    s_idx = lax.broadcasted_iota(jnp.int32, (CHUNK, CHUNK), 0)
    t_idx = lax.broadcasted_iota(jnp.int32, (CHUNK, CHUNK), 1)
    causal = t_idx >= s_idx
    pos = lax.broadcasted_iota(jnp.int32, (n_chunks, 2 * STATE_DIM), 0) & (chunks_per_seq - 1)

    def build_slab(taps_ref, slot, c):
        def rows_j(j, carry):
            for e in range(per_slab):
                row = j * n + c * per_slab + e
                taps = jnp.broadcast_to(taps_ref[pl.ds(row, 1), :], (CHUNK, CHUNK))
                shifted = pltpu.roll(taps, 0, 1, stride=1, stride_axis=0)
                t_ref[slot, c, pl.ds(pl.multiple_of(j * CHUNK, CHUNK), CHUNK),
                      e * CHUNK:(e + 1) * CHUNK] = jnp.where(causal, shifted, 0.0).astype(BF16)
            return carry
        lax.fori_loop(0, n, rows_j, 0, unroll=True)

    def shift_rows(x, k):
        return jnp.where(pos >= k, pltpu.roll(x, k, 0), 0.0)

    def group(q, next_taps_ref):
        u = u_ref[q]
        ub = u.astype(BF16)
        b = jnp.dot(ub, w_ref[q], preferred_element_type=F32)
        h = shift_rows(b, 1)
        for k in range(int(math.log2(chunks_per_seq))):
            hs = shift_rows(h, 1 << k)
            h = (h + hs * ap1_ref[q, k:k + 1, :]
                 + pltpu.roll(hs, STATE_DIM, 1) * ap2_ref[q, k:k + 1, :])
        hv = jnp.dot(h.astype(BF16), v_ref[q], preferred_element_type=F32)

        def rebuild(c, carry):
            build_slab(next_taps_ref, 1 - q, c)
            return carry
        lax.fori_loop(0, n_slabs, rebuild, 0, unroll=True)

        pitch = n_chunks + STAGE_PAD
        base = q * n * pitch
        for c in range(n_slabs):
            yc = jnp.dot(ub, t_ref[q, c], preferred_element_type=F32)
            for e in range(per_slab):
                i = c * per_slab + e
                cols = slice(i * CHUNK, (i + 1) * CHUNK)
                yi = yc[:, e * CHUNK:(e + 1) * CHUNK] + hv[:, cols] + d_ref[q, :, cols] * u[:, cols]
                ystage[base + i * pitch:base + i * pitch + n_chunks, :] = jax.nn.gelu(yi)
        for c in range(n_chunks):
            for i0 in range(0, n, 8):
                yt_ref[q * n + i0:q * n + i0 + 8, c * CHUNK:(c + 1) * CHUNK] = (
                    ystage[pl.ds(base + i0 * pitch + c, 8, stride=pitch), :])

    @pl.when(pl.program_id(0) == 0)
    def _():
        def first(c, carry):
            build_slab(krow_ref.at[0], 0, c)
            return carry
        lax.fori_loop(0, n_slabs, first, 0)

    group(0, krow_ref.at[1])
    group(1, knext_ref)


def _s5_main(u_g, krow, w, v, ap1, ap2, d_rep, chunks_per_seq):
    g, n_chunks, width = u_g.shape
    gs = GROUPS_PER_STEP
    assert chunks_per_seq & (chunks_per_seq - 1) == 0 and chunks_per_seq <= 1 << N_DOUBLINGS
    assert g % gs == 0 and gs == 2
    spec = lambda *shape: pl.BlockSpec((gs,) + shape, lambda i: (i,) + (0,) * len(shape))
    taps = (SSM_GROUP * SSM_GROUP, CHUNK)
    return pl.pallas_call(
        functools.partial(_s5_kernel, chunks_per_seq=chunks_per_seq),
        grid=(g // gs,),
        in_specs=[spec(n_chunks, width), spec(*taps),
                  pl.BlockSpec((None,) + taps, lambda i: (jnp.minimum(gs * i + gs, g - 1), 0, 0)),
                  spec(width, 2 * STATE_DIM), spec(2 * STATE_DIM, width),
                  spec(N_DOUBLINGS, 2 * STATE_DIM), spec(N_DOUBLINGS, 2 * STATE_DIM),
                  spec(1, width)],
        out_specs=pl.BlockSpec((gs * SSM_GROUP, n_chunks * CHUNK), lambda i: (i, 0)),
        out_shape=jax.ShapeDtypeStruct((g * SSM_GROUP, n_chunks * CHUNK), F32),
        scratch_shapes=[pltpu.VMEM((gs, width // T_COLS, width, T_COLS), BF16),
                        pltpu.VMEM((gs * SSM_GROUP * (n_chunks + STAGE_PAD), CHUNK), F32)],
        compiler_params=_params(1),
        name="s5_main",
    )(u_g, krow, krow, w, v, ap1, ap2, d_rep)


def _glu_kernel(yt_ref, wt_ref, b_ref, g_ref, o_ref):
    y = yt_ref[...]
    z = jnp.dot(wt_ref[...], y.astype(BF16), preferred_element_type=F32) + b_ref[...]
    ssm = y * jax.nn.sigmoid(z)
    inv = lax.rsqrt(jnp.mean(ssm * ssm, axis=0, keepdims=True) + EPS)
    o_ref[...] = (ssm * inv * g_ref[...]).astype(BF16)


def _glu(yt, w_glu_t_bf, b_glu, ssm_out_g):
    w, t = yt.shape
    tn = GLU_TOKENS
    return pl.pallas_call(
        _glu_kernel,
        grid=(t // tn,),
        in_specs=[pl.BlockSpec((w, tn), lambda m: (0, m)),
                  pl.BlockSpec((w, w), lambda m: (0, 0)),
                  pl.BlockSpec((w, 1), lambda m: (0, 0)),
                  pl.BlockSpec((w, 1), lambda m: (0, 0))],
        out_specs=pl.BlockSpec((w, tn), lambda m: (0, m)),
        out_shape=jax.ShapeDtypeStruct((w, t), BF16),
        compiler_params=_params(1),
        name="glu",
    )(yt, w_glu_t_bf, b_glu.reshape(w, 1), ssm_out_g.reshape(w, 1))


OUT_COLS = 512


def _outproj_kernel(attn_ref, ssmt_ref, x_ref, mod_ref, ag_ref, wa_ref, ws_ref, g2_ref,
                    x1_ref, h2_ref):
    d = x_ref.shape[1]
    attn_n = (_rms(attn_ref[...].astype(F32)) * ag_ref[...]).astype(BF16)
    ssmt = ssmt_ref[...]
    sq = jnp.zeros((x_ref.shape[0], 1), F32)
    for c0 in range(0, d, OUT_COLS):
        cols = slice(c0, c0 + OUT_COLS)
        mixed = jnp.dot(attn_n, wa_ref[:, cols], preferred_element_type=F32)
        mixed = mixed + lax.dot_general(ssmt, ws_ref[:, cols], (((0,), (0,)), ((), ())),
                                        preferred_element_type=F32)
        x1 = x_ref[:, cols] + mod_ref[0, 2:3, cols] * mixed
        x1_ref[:, cols] = x1
        sq = sq + jnp.sum(x1 * x1, axis=-1, keepdims=True)
    inv = lax.rsqrt(sq * (1.0 / d) + EPS)
    for c0 in range(0, d, OUT_COLS):
        cols = slice(c0, c0 + OUT_COLS)
        gain = g2_ref[:, cols] * (1.0 + mod_ref[0, 4:5, cols])
        h2_ref[:, cols] = (x1_ref[:, cols] * inv * gain + mod_ref[0, 3:4, cols]).astype(BF16)


def _outproj(attn2, ssmt, x2, mod3, attn_out_g, w_out_bf, norm2_g, seq):
    t, d = x2.shape
    wa = attn2.shape[1]
    ws = ssmt.shape[0]
    tm = OUTPROJ_ROWS
    per_batch = seq // tm
    return pl.pallas_call(
        _outproj_kernel,
        grid=(t // tm,),
        in_specs=[pl.BlockSpec((tm, wa), lambda m: (m, 0)),
                  pl.BlockSpec((ws, tm), lambda m: (0, m)),
                  pl.BlockSpec((tm, d), lambda m: (m, 0)),
                  pl.BlockSpec((1, N_MOD, d), lambda m: (m // per_batch, 0, 0)),
                  pl.BlockSpec((1, wa), lambda m: (0, 0)),
                  pl.BlockSpec((wa, d), lambda m: (0, 0)),
                  pl.BlockSpec((ws, d), lambda m: (1, 0)),
                  pl.BlockSpec((1, d), lambda m: (0, 0))],
        out_specs=[pl.BlockSpec((tm, d), lambda m: (m, 0)),
                   pl.BlockSpec((tm, d), lambda m: (m, 0))],
        out_shape=[jax.ShapeDtypeStruct((t, d), F32),
                   jax.ShapeDtypeStruct((t, d), BF16)],
        compiler_params=_params(1),
        name="outproj",
    )(attn2, ssmt, x2, mod3, attn_out_g.reshape(1, wa), w_out_bf, w_out_bf,
      norm2_g.reshape(1, d))


def _ffn_kernel(h_ref, w1_ref, w2_ref, x1_ref, mod_ref, o_ref):
    f = pl.program_id(1)

    def partial_sum():
        a = jnp.dot(h_ref[...], w1_ref[...], preferred_element_type=F32)
        a = jnp.square(jnp.maximum(a, 0.0)).astype(BF16)
        return jnp.dot(a, w2_ref[...], preferred_element_type=F32)

    @pl.when(f == 0)
    def _():
        o_ref[...] = partial_sum()

    @pl.when(f > 0)
    def _():
        o_ref[...] += partial_sum()

    @pl.when(f == pl.num_programs(1) - 1)
    def _():
        o_ref[...] = x1_ref[...] + mod_ref[0, 5:6, :] * o_ref[...]


def _ffn(h2, w1_bf, w2_bf, x1, mod3, seq):
    t, d = x1.shape
    dff = w1_bf.shape[1]
    tm = FFN_ROWS
    tf = FFN_HIDDEN
    per_batch = seq // tm
    return pl.pallas_call(
        _ffn_kernel,
        grid=(t // tm, dff // tf),
        in_specs=[pl.BlockSpec((tm, d), lambda m, f: (m, 0)),
                  pl.BlockSpec((d, tf), lambda m, f: (0, f)),
                  pl.BlockSpec((tf, d), lambda m, f: (f, 0)),
                  pl.BlockSpec((tm, d), lambda m, f: (m, 0)),
                  pl.BlockSpec((1, N_MOD, d), lambda m, f: (m // per_batch, 0, 0))],
        out_specs=pl.BlockSpec((tm, d), lambda m, f: (m, 0)),
        out_shape=jax.ShapeDtypeStruct((t, d), F32),
        compiler_params=_params(2),
        name="ffn",
    )(h2, w1_bf, w2_bf, x1, mod3)


def _layer(x, c, w_ada, b_ada, norm1_g, w_in, q_norm_g, k_norm_g, lam_re, lam_im, log_step,
           b_re, b_im, c_re, c_im, d_skip, w_glu, b_glu, attn_out_g, ssm_out_g, w_out,
           norm2_g, w_ff1, w_ff2):
    bsz, seq, d = x.shape
    t = bsz * seq
    n_groups = lam_re.shape[0]
    chunks_per_seq = seq // CHUNK

    mod3 = _adaln(c, w_ada, b_ada).reshape(bsz, N_MOD, d)
    x2 = x.reshape(t, d)
    w_qkv = w_in[:, :3 * ATTN_WIDTH].astype(BF16)
    w_ut = w_in[:, 3 * ATTN_WIDTH:].T.astype(BF16)
    qkv, u_g = _inproj(x2, mod3, norm1_g, w_qkv, w_ut, q_norm_g, k_norm_g, seq)

    slopes = 2.0 ** (-8.0 * (jnp.arange(N_HEADS, dtype=F32) + 1.0) / N_HEADS)
    slopes = jnp.broadcast_to(slopes[:, None, None], (N_HEADS, 1, 2 * BAND))
    attn = _attention(qkv.reshape(bsz, seq, 3 * ATTN_WIDTH), slopes)

    krow, w_s, v_s, ap1, ap2 = _s5_params(lam_re, lam_im, log_step, b_re, b_im, c_re, c_im)
    d_rep = jnp.repeat(d_skip.reshape(n_groups, SSM_GROUP), CHUNK, axis=1)
    yt = _s5_main(u_g, krow, w_s, v_s, ap1, ap2, d_rep.reshape(n_groups, 1, SSM_GROUP * CHUNK),
                  chunks_per_seq)
    ssm = _glu(yt, w_glu.T.astype(BF16), b_glu, ssm_out_g)
    x1, h2 = _outproj(attn.reshape(t, ATTN_WIDTH), ssm, x2, mod3, attn_out_g,
                      w_out.astype(BF16), norm2_g, seq)
    out = _ffn(h2, w_ff1.astype(BF16), w_ff2.astype(BF16), x1, mod3, seq)
    return out.reshape(bsz, seq, d)


def kernel(x, c, w_ada, b_ada, norm1_g, w_in, q_norm_g, k_norm_g, lam_re, lam_im, log_step,
           b_re, b_im, c_re, c_im, d_skip, w_glu, b_glu, attn_out_g, ssm_out_g, w_out,
           norm2_g, w_ff1, w_ff2):
    for l in range(w_ada.shape[0]):
        x = _layer(x, c, w_ada[l], b_ada[l], norm1_g[l], w_in[l], q_norm_g[l], k_norm_g[l],
                   lam_re[l], lam_im[l], log_step[l], b_re[l], b_im[l], c_re[l], c_im[l],
                   d_skip[l], w_glu[l], b_glu[l], attn_out_g[l], ssm_out_g[l], w_out[l],
                   norm2_g[l], w_ff1[l], w_ff2[l])
    return x
```

```python
import functools
import math

import jax
import jax.numpy as jnp
from jax import lax
from jax.experimental import pallas as pl
from jax.experimental.pallas import tpu as pltpu

F32 = jnp.float32
BF16 = jnp.bfloat16

HEAD_DIM = 128
N_HEADS = 8
ATTN_WIDTH = N_HEADS * HEAD_DIM
SSM_GROUP = 16
STATE_DIM = 64
CHUNK = 128
BAND = 128
DILATIONS = (1, 4, 16)
N_MOD = 6
EPS = 1e-6
NEG = -1e30
LOG2E = math.log2(math.e)

V7X_VMEM_BYTES = 64 * 1024 * 1024
V7X_SUBLANES = 8
VMEM_LIMIT = V7X_VMEM_BYTES - 8 * 1024 * 1024

ADALN_COLS = 1024
INPROJ_ROWS = 1024
GLU_TOKENS = 1024
OUTPROJ_ROWS = 512
FFN_ROWS = 512
FFN_HIDDEN = 1024


def _params(n_axes):
    return pltpu.CompilerParams(
        dimension_semantics=("arbitrary",) * n_axes, vmem_limit_bytes=VMEM_LIMIT)


def _rms(x, eps=EPS):
    return x * lax.rsqrt(jnp.mean(x * x, axis=-1, keepdims=True) + eps)


def _adaln_kernel(c_ref, w_ref, b_ref, o_ref):
    c = c_ref[...]
    s = (c * jax.nn.sigmoid(c)).astype(BF16)
    o_ref[...] = jnp.dot(s, w_ref[...].astype(BF16), preferred_element_type=F32) + b_ref[...]


def _adaln(c, w_ada, b_ada):
    bsz, d = c.shape
    n_out = w_ada.shape[1]
    rows = V7X_SUBLANES
    tn = ADALN_COLS
    assert bsz <= rows
    c8 = jnp.zeros((rows, d), F32).at[:bsz].set(c)
    out = pl.pallas_call(
        _adaln_kernel,
        grid=(n_out // tn,),
        in_specs=[pl.BlockSpec((rows, d), lambda n: (0, 0)),
                  pl.BlockSpec((d, tn), lambda n: (0, n)),
                  pl.BlockSpec((1, tn), lambda n: (0, n))],
        out_specs=pl.BlockSpec((rows, tn), lambda n: (0, n)),
        out_shape=jax.ShapeDtypeStruct((rows, n_out), F32),
        compiler_params=_params(1),
        name="adaln",
    )(c8, w_ada, b_ada.reshape(1, n_out))
    return out[:bsz]


STAGE_PAD = 4


def _inproj_kernel(x_ref, mod_ref, g1_ref, w_ref, wu_ref, qg_ref, kg_ref, qkv_ref, u_ref,
                   h_ref, ustage):
    n = pl.program_id(1)
    tm = h_ref.shape[0]

    @pl.when(n == 0)
    def _():
        shift = mod_ref[0, 0:1, :]
        scale = mod_ref[0, 1:2, :]
        y = _rms(x_ref[...]) * g1_ref[...]
        h_ref[...] = (y * (1.0 + scale) + shift).astype(BF16)

    def head_norm(gain):
        for pair in range(N_HEADS // 2):
            c0 = pair * 2 * HEAD_DIM
            acc = jnp.dot(h_ref[...], w_ref[:, c0:c0 + 2 * HEAD_DIM], preferred_element_type=F32)
            for hd in range(2):
                sl = slice(hd * HEAD_DIM, (hd + 1) * HEAD_DIM)
                qkv_ref[:, c0 + hd * HEAD_DIM:c0 + (hd + 1) * HEAD_DIM] = (
                    _rms(acc[:, sl]) * gain).astype(BF16)

    @pl.when(n == 0)
    def _():
        head_norm(qg_ref[...] * (HEAD_DIM ** -0.5 * LOG2E))

    @pl.when(n == 1)
    def _():
        head_norm(kg_ref[...])

    @pl.when(n == 2)
    def _():
        qkv_ref[...] = jnp.dot(h_ref[...], w_ref[...], preferred_element_type=F32).astype(BF16)

    @pl.when(n == 3)
    def _():
        width = wu_ref.shape[1]
        pitch = width + STAGE_PAD
        sub = 2 * CHUNK
        for c2 in range(tm // sub):
            ut = jnp.dot(h_ref[c2 * sub:(c2 + 1) * sub, :], wu_ref[...],
                         preferred_element_type=F32).T
            for cc in range(sub // CHUNK):
                c = c2 * (sub // CHUNK) + cc
                ustage[c * pitch:c * pitch + width, :] = ut[:, cc * CHUNK:(cc + 1) * CHUNK]

        def regroup(g, carry):
            for j in range(SSM_GROUP):
                u_ref[g, :, j * CHUNK:(j + 1) * CHUNK] = (
                    ustage[pl.ds(g * SSM_GROUP + j, tm // CHUNK, stride=pitch), :])
            return carry
        lax.fori_loop(0, width // SSM_GROUP, regroup, 0)


def _inproj(x2, mod3, norm1_g, w_qkv_bf, w_u_bf, q_norm_g, k_norm_g, seq):
    t, d = x2.shape
    tm = INPROJ_ROWS
    tn = ATTN_WIDTH
    width = w_u_bf.shape[1]
    n_groups = width // SSM_GROUP
    assert w_qkv_bf.shape[1] == 3 * tn and seq % tm == 0 and tm // CHUNK == V7X_SUBLANES
    per_batch = seq // tm
    return pl.pallas_call(
        _inproj_kernel,
        grid=(t // tm, 4),
        in_specs=[pl.BlockSpec((tm, d), lambda m, n: (m, 0)),
                  pl.BlockSpec((1, N_MOD, d), lambda m, n: (m // per_batch, 0, 0)),
                  pl.BlockSpec((1, d), lambda m, n: (0, 0)),
                  pl.BlockSpec((d, tn), lambda m, n: (0, jnp.minimum(n, 2))),
                  pl.BlockSpec((d, width), lambda m, n: (0, 0), pipeline_mode=pl.Buffered(1)),
                  pl.BlockSpec((1, HEAD_DIM), lambda m, n: (0, 0)),
                  pl.BlockSpec((1, HEAD_DIM), lambda m, n: (0, 0))],
        out_specs=[pl.BlockSpec((tm, tn), lambda m, n: (m, jnp.minimum(n, 2))),
                   pl.BlockSpec((n_groups, tm // CHUNK, SSM_GROUP * CHUNK), lambda m, n: (0, m, 0))],
        out_shape=[jax.ShapeDtypeStruct((t, 3 * tn), BF16),
                   jax.ShapeDtypeStruct((n_groups, t // CHUNK, SSM_GROUP * CHUNK), F32)],
        scratch_shapes=[pltpu.VMEM((tm, d), BF16),
                        pltpu.VMEM((tm // CHUNK * (width + STAGE_PAD), CHUNK), F32)],
        compiler_params=_params(2),
        name="inproj",
    )(x2, mod3, norm1_g.reshape(1, d), w_qkv_bf, w_u_bf, q_norm_g.reshape(1, HEAD_DIM),
      k_norm_g.reshape(1, HEAD_DIM))


ATTN_SUPER = BAND * max(DILATIONS)
BLOCKS_PER_SUPER = ATTN_SUPER // BAND
WIDE_DILATION = 16
WIDE_PITCH = BAND + 8


def _attn_kernel(slope_ref, q_ref, k_ref, v_ref, o_ref,
                 stage, stage4, q4, q16, k1, k4, k16, v1, v4, v16,
                 bias_ref, oacc, mstat, lstat, *, seq):
    assert DILATIONS == (1, 4, 16)
    n_pat = len(DILATIONS)
    n4 = ATTN_SUPER // 4
    n16 = ATTN_SUPER // 16

    def deinterleave(src, a1, a4, a16, pad):
        def chunk(c, carry):
            r0 = pl.multiple_of(c * ATTN_SUPER, ATTN_SUPER)
            x = src[pl.ds(r0, ATTN_SUPER), :]
            if a1 is not None:
                a1[0, pl.ds(pad + r0, ATTN_SUPER), :] = x
            stage[...] = x.astype(F32)
            for r in range(4):
                y = stage[pl.ds(r, n4, stride=4), :]
                stage4[r * n4:(r + 1) * n4, :] = y
                a4[r, pl.ds(pad + pl.multiple_of(c * n4, n4), n4), :] = y.astype(BF16)
            for r in range(4):
                for a in range(4):
                    z = stage4[pl.ds(r * n4 + a, n16, stride=4), :]
                    a16[4 * a + r, pl.ds(pad + pl.multiple_of(c * n16, n16), n16), :] = (
                        z.astype(BF16))
            return carry
        lax.fori_loop(0, seq // ATTN_SUPER, chunk, 0)

    for buf in (k1, k4, k16, v1, v4, v16):
        buf[:, 0:BAND, :] = jnp.zeros((buf.shape[0], BAND, HEAD_DIM), BF16)
    deinterleave(q_ref, None, q4, q16, 0)
    deinterleave(k_ref, k1, k4, k16, BAND)
    deinterleave(v_ref, v1, v4, v16, BAND)

    qi = lax.broadcasted_iota(jnp.int32, (BAND, 2 * BAND), 0)
    ki = lax.broadcasted_iota(jnp.int32, (BAND, 2 * BAND), 1)
    steps = qi - ki + BAND
    valid = (steps >= 0) & (steps <= BAND)
    slope = slope_ref[...]
    for p, d in enumerate(DILATIONS):
        bias = jnp.where(valid, -(slope * (float(d) * LOG2E)) * steps.astype(F32), NEG)
        bias_ref[2 * p + 1] = bias
        bias_ref[2 * p] = jnp.where(ki >= BAND, bias, NEG)

    q_streams = (None, q4, q16)
    k_streams = (k1, k4, k16)
    v_streams = (v1, v4, v16)
    ones = jnp.ones((2 * BAND, HEAD_DIM), BF16)

    def rows(start, size, d):
        return pl.ds(start, size) if d == 1 else pl.ds(start, size, stride=d)

    def block(p, d, r, j, local):
        row0 = pl.multiple_of(j * BAND, BAND)
        if d == 1:
            q = q_ref[pl.ds(row0, BAND), :]
        else:
            q = q_streams[p][r, pl.ds(row0, BAND), :]
        k2 = k_streams[p][r, pl.ds(row0, 2 * BAND), :]
        v2 = v_streams[p][r, pl.ds(row0, 2 * BAND), :]
        s = lax.dot_general(q, k2, (((1,), (1,)), ((), ())), preferred_element_type=F32)
        has_prev = jnp.minimum(j, 1)
        s = s + bias_ref[2 * p + has_prev]
        m = jnp.max(s, axis=-1, keepdims=True)
        e = jnp.exp2(s - m).astype(BF16)
        acc = jnp.dot(e, jnp.concatenate([v2, ones], axis=1), preferred_element_type=F32)
        if d == WIDE_DILATION:
            dst = pl.ds(pl.multiple_of(r * WIDE_PITCH, 8), BAND)
        else:
            dst = rows(local, BAND, d)
        oacc[p, dst, :] = acc[:, :HEAD_DIM]
        lstat[p, dst, :] = acc[:, HEAD_DIM:]
        mstat[p, dst, :] = jnp.broadcast_to(m, (BAND, HEAD_DIM))

    merge_rows = 256

    def super_block(sb, carry):
        p0 = sb * ATTN_SUPER

        def blocks(idx, c2):
            for p, d in enumerate(DILATIONS):
                r = idx & (d - 1)
                jl = idx >> int(math.log2(d))
                block(p, d, r, sb * (BLOCKS_PER_SUPER // d) + jl, r + BAND * d * jl)
            return c2
        lax.fori_loop(0, BLOCKS_PER_SUPER, blocks, 0, unroll=True)

        def merge(c, c2):
            r = pl.multiple_of(c * merge_rows, merge_rows)

            def natural(buf, p):
                if DILATIONS[p] != WIDE_DILATION:
                    return buf[p, pl.ds(r, merge_rows), :]
                i0 = c * (merge_rows // WIDE_DILATION)
                return jnp.concatenate(
                    [buf[p, pl.ds(i0 + i + half * 8 * WIDE_PITCH, 8, stride=WIDE_PITCH), :]
                     for i in range(merge_rows // WIDE_DILATION)
                     for half in range(WIDE_DILATION // 8)], axis=0)

            ms = [natural(mstat, p) for p in range(n_pat)]
            mx = functools.reduce(jnp.maximum, ms)
            num = jnp.zeros((merge_rows, HEAD_DIM), F32)
            den = jnp.zeros((merge_rows, HEAD_DIM), F32)
            for p in range(n_pat):
                w = jnp.exp2(ms[p] - mx)
                num = num + w * natural(oacc, p)
                den = den + w * natural(lstat, p)
            o_ref[pl.ds(pl.multiple_of(p0 + r, merge_rows), merge_rows), :] = (num / den).astype(BF16)
            return c2
        lax.fori_loop(0, ATTN_SUPER // merge_rows, merge, 0)
        return carry
    lax.fori_loop(0, seq // ATTN_SUPER, super_block, 0)


def _attention(qkv3, slopes):
    bsz, seq, _ = qkv3.shape
    assert seq % ATTN_SUPER == 0
    blk = lambda off: pl.BlockSpec((None, seq, HEAD_DIM), lambda b, h: (b, 0, off + h))
    stream = lambda d, pad: pltpu.VMEM((d, pad + seq // d, HEAD_DIM), BF16)
    stat = pltpu.VMEM((len(DILATIONS), WIDE_DILATION * WIDE_PITCH, HEAD_DIM), F32)
    return pl.pallas_call(
        functools.partial(_attn_kernel, seq=seq),
        grid=(bsz, N_HEADS),
        in_specs=[pl.BlockSpec((None, 1, 2 * BAND), lambda b, h: (h, 0, 0)),
                  blk(0), blk(N_HEADS), blk(2 * N_HEADS)],
        out_specs=pl.BlockSpec((None, seq, HEAD_DIM), lambda b, h: (b, 0, h)),
        out_shape=jax.ShapeDtypeStruct((bsz, seq, ATTN_WIDTH), BF16),
        scratch_shapes=[pltpu.VMEM((ATTN_SUPER, HEAD_DIM), F32),
                        pltpu.VMEM((ATTN_SUPER, HEAD_DIM), F32),
                        stream(4, 0), stream(16, 0),
                        stream(1, BAND), stream(4, BAND), stream(16, BAND),
                        stream(1, BAND), stream(4, BAND), stream(16, BAND),
                        pltpu.VMEM((2 * len(DILATIONS), BAND, 2 * BAND), F32),
                        stat, stat, stat],
        compiler_params=_params(2),
        name="attention",
    )(slopes, qkv3, qkv3, qkv3)


N_DOUBLINGS = 8


def _cmul(ar, ai, br, bi):
    return ar * br - ai * bi, ar * bi + ai * br


def _cpow(ar, ai, e, nbits):
    shape = jnp.broadcast_shapes(ar.shape, e.shape)
    pr = jnp.ones(shape, F32)
    pi = jnp.zeros(shape, F32)
    br, bi = ar, ai
    for k in range(nbits):
        nr, ni = _cmul(pr, pi, br, bi)
        sel = ((e >> k) & 1) == 1
        pr = jnp.where(sel, nr, pr)
        pi = jnp.where(sel, ni, pi)
        br, bi = _cmul(br, bi, br, bi)
    return pr, pi


def _zoh(lr, li, log_step):
    dt = jnp.exp(log_step)
    mag = jnp.exp(lr * dt)
    ar = mag * jnp.cos(li * dt)
    ai = mag * jnp.sin(li * dt)
    den = lr * lr + li * li
    cr = ((ar - 1.0) * lr + ai * li) / den
    ci = (ai * lr - (ar - 1.0) * li) / den
    return ar, ai, cr, ci


def _s5_params_kernel(lr_row, li_row, ls_row, lr_col, li_col, ls_col,
                      bt_re, bt_im, c_re, c_im, ct_re, ct_im,
                      krow_ref, w_ref, v_ref, ap1_ref, ap2_ref):
    p_dim = STATE_DIM
    ar, ai, cr, ci = _zoh(lr_row[...], li_row[...], ls_row[...])
    bbr, bbi = _cmul(cr, ci, bt_re[...], bt_im[...])
    s_idx = lax.broadcasted_iota(jnp.int32, (CHUNK, p_dim), 0)
    qr, qi = _cpow(ar, ai, CHUNK - 1 - s_idx, 7)
    for j in range(SSM_GROUP):
        wr, wi = _cmul(qr, qi, bbr[j:j + 1, :], bbi[j:j + 1, :])
        w_ref[j * CHUNK:(j + 1) * CHUNK, :] = jnp.concatenate([wr, wi], axis=-1).astype(BF16)
    mr, mi = ar, ai
    for _ in range(7):
        mr, mi = _cmul(mr, mi, mr, mi)
    for k in range(N_DOUBLINGS):
        ap1_ref[k:k + 1, :] = jnp.concatenate([mr, mr], axis=-1)
        ap2_ref[k:k + 1, :] = jnp.concatenate([-mi, mi], axis=-1)
        mr, mi = _cmul(mr, mi, mr, mi)

    acr, aci, _, _ = _zoh(lr_col[...], li_col[...], ls_col[...])
    t_idx = lax.broadcasted_iota(jnp.int32, (p_dim, CHUNK), 1)
    pr, pi = _cpow(acr, aci, t_idx, 7)
    cbr, cbi = [], []
    for j in range(SSM_GROUP):
        r_, i_ = _cmul(c_re[...], c_im[...], bbr[j:j + 1, :], bbi[j:j + 1, :])
        cbr.append(r_)
        cbi.append(i_)
    cbr = jnp.concatenate(cbr, axis=0)
    cbi = jnp.concatenate(cbi, axis=0)
    hi = lax.Precision.HIGHEST
    krow_ref[...] = (jnp.dot(cbr, pr, precision=hi, preferred_element_type=F32)
                     - jnp.dot(cbi, pi, precision=hi, preferred_element_type=F32))
    p1r, p1i = _cmul(pr, pi, acr, aci)
    ctr = ct_re[...]
    cti = ct_im[...]
    for i in range(SSM_GROUP):
        vr, vi = _cmul(ctr[:, i:i + 1], cti[:, i:i + 1], p1r, p1i)
        v_ref[0:p_dim, i * CHUNK:(i + 1) * CHUNK] = vr.astype(BF16)
        v_ref[p_dim:2 * p_dim, i * CHUNK:(i + 1) * CHUNK] = (-vi).astype(BF16)


def _s5_params(lam_re, lam_im, log_step, b_re, b_im, c_re, c_im):
    g, p = lam_re.shape
    n = SSM_GROUP
    row = lambda a: a.reshape(g, 1, p)
    col = lambda a: a.reshape(g, p, 1)
    ls = jnp.broadcast_to(log_step[:, None], (g, p))
    tr = lambda a: jnp.swapaxes(a, 1, 2)
    spec = lambda *shape: pl.BlockSpec((None,) + shape, lambda i: (i,) + (0,) * len(shape))
    return pl.pallas_call(
        _s5_params_kernel,
        grid=(g,),
        in_specs=[spec(1, p)] * 3 + [spec(p, 1)] * 3 + [spec(n, p)] * 4 + [spec(p, n)] * 2,
        out_specs=[spec(n * n, CHUNK), spec(n * CHUNK, 2 * p), spec(2 * p, n * CHUNK),
                   spec(N_DOUBLINGS, 2 * p), spec(N_DOUBLINGS, 2 * p)],
        out_shape=[jax.ShapeDtypeStruct((g, n * n, CHUNK), F32),
                   jax.ShapeDtypeStruct((g, n * CHUNK, 2 * p), BF16),
                   jax.ShapeDtypeStruct((g, 2 * p, n * CHUNK), BF16),
                   jax.ShapeDtypeStruct((g, N_DOUBLINGS, 2 * p), F32),
                   jax.ShapeDtypeStruct((g, N_DOUBLINGS, 2 * p), F32)],
        compiler_params=_params(1),
        name="s5_params",
    )(row(lam_re), row(lam_im), row(ls), col(lam_re), col(lam_im), col(ls),
      tr(b_re), tr(b_im), c_re, c_im, tr(c_re), tr(c_im))


GROUPS_PER_STEP = 2
T_COLS = 4 * CHUNK


def _s5_kernel(u_ref, krow_ref, knext_ref, w_ref, v_ref, ap1_ref, ap2_ref, d_ref, yt_ref,
               t_ref, ystage, *, chunks_per_seq):
    n = SSM_GROUP
    n_chunks = u_ref.shape[1]
    n_slabs = n * CHUNK // T_COLS
    per_slab = T_COLS // CHUNK

    s_idx = lax.broadcasted_iota(jnp.int32, (CHUNK, CHUNK), 0)
    t_idx = lax.broadcasted_iota(jnp.int32, (CHUNK, CHUNK), 1)
    causal = t_idx >= s_idx
    pos = lax.broadcasted_iota(jnp.int32, (n_chunks, 2 * STATE_DIM), 0) & (chunks_per_seq - 1)

    def build_slab(taps_ref, slot, c):
        def rows_j(j, carry):
            for e in range(per_slab):
                row = j * n + c * per_slab + e
                taps = jnp.broadcast_to(taps_ref[pl.ds(row, 1), :], (CHUNK, CHUNK))
                shifted = pltpu.roll(taps, 0, 1, stride=1, stride_axis=0)
                t_ref[slot, c, pl.ds(pl.multiple_of(j * CHUNK, CHUNK), CHUNK),
                      e * CHUNK:(e + 1) * CHUNK] = jnp.where(causal, shifted, 0.0).astype(BF16)
            return carry
        lax.fori_loop(0, n, rows_j, 0, unroll=True)

    def shift_rows(x, k):
        return jnp.where(pos >= k, pltpu.roll(x, k, 0), 0.0)

    def group(q, next_taps_ref):
        u = u_ref[q]
        ub = u.astype(BF16)
        b = jnp.dot(ub, w_ref[q], preferred_element_type=F32)
        h = shift_rows(b, 1)
        for k in range(int(math.log2(chunks_per_seq))):
            hs = shift_rows(h, 1 << k)
            h = (h + hs * ap1_ref[q, k:k + 1, :]
                 + pltpu.roll(hs, STATE_DIM, 1) * ap2_ref[q, k:k + 1, :])
        hv = jnp.dot(h.astype(BF16), v_ref[q], preferred_element_type=F32)

        def rebuild(c, carry):
            build_slab(next_taps_ref, 1 - q, c)
            return carry
        lax.fori_loop(0, n_slabs, rebuild, 0, unroll=True)

        pitch = n_chunks + STAGE_PAD
        base = q * n * pitch
        for c in range(n_slabs):
            yc = jnp.dot(ub, t_ref[q, c], preferred_element_type=F32)
            for e in range(per_slab):
                i = c * per_slab + e
                cols = slice(i * CHUNK, (i + 1) * CHUNK)
                yi = yc[:, e * CHUNK:(e + 1) * CHUNK] + hv[:, cols] + d_ref[q, :, cols] * u[:, cols]
                ystage[base + i * pitch:base + i * pitch + n_chunks, :] = jax.nn.gelu(yi)
        for c in range(n_chunks):
            for i0 in range(0, n, 8):
                yt_ref[q * n + i0:q * n + i0 + 8, c * CHUNK:(c + 1) * CHUNK] = (
                    ystage[pl.ds(base + i0 * pitch + c, 8, stride=pitch), :])

    @pl.when(pl.program_id(0) == 0)
    def _():
        def first(c, carry):
            build_slab(krow_ref.at[0], 0, c)
            return carry
        lax.fori_loop(0, n_slabs, first, 0)

    group(0, krow_ref.at[1])
    group(1, knext_ref)


def _s5_main(u_g, krow, w, v, ap1, ap2, d_rep, chunks_per_seq):
    g, n_chunks, width = u_g.shape
    gs = GROUPS_PER_STEP
    assert chunks_per_seq & (chunks_per_seq - 1) == 0 and chunks_per_seq <= 1 << N_DOUBLINGS
    assert g % gs == 0 and gs == 2
    spec = lambda *shape: pl.BlockSpec((gs,) + shape, lambda i: (i,) + (0,) * len(shape))
    taps = (SSM_GROUP * SSM_GROUP, CHUNK)
    return pl.pallas_call(
        functools.partial(_s5_kernel, chunks_per_seq=chunks_per_seq),
        grid=(g // gs,),
        in_specs=[spec(n_chunks, width), spec(*taps),
                  pl.BlockSpec((None,) + taps, lambda i: (jnp.minimum(gs * i + gs, g - 1), 0, 0)),
                  spec(width, 2 * STATE_DIM), spec(2 * STATE_DIM, width),
                  spec(N_DOUBLINGS, 2 * STATE_DIM), spec(N_DOUBLINGS, 2 * STATE_DIM),
                  spec(1, width)],
        out_specs=pl.BlockSpec((gs * SSM_GROUP, n_chunks * CHUNK), lambda i: (i, 0)),
        out_shape=jax.ShapeDtypeStruct((g * SSM_GROUP, n_chunks * CHUNK), F32),
        scratch_shapes=[pltpu.VMEM((gs, width // T_COLS, width, T_COLS), BF16),
                        pltpu.VMEM((gs * SSM_GROUP * (n_chunks + STAGE_PAD), CHUNK), F32)],
        compiler_params=_params(1),
        name="s5_main",
    )(u_g, krow, krow, w, v, ap1, ap2, d_rep)


def _glu_kernel(yt_ref, w_ref, b_ref, g_ref, o_ref):
    y = yt_ref[...]
    z = lax.dot_general(w_ref[...], y.astype(BF16), (((0,), (0,)), ((), ())),
                        preferred_element_type=F32) + b_ref[...]
    ssm = y * jax.nn.sigmoid(z)
    inv = lax.rsqrt(jnp.mean(ssm * ssm, axis=0, keepdims=True) + EPS)
    o_ref[...] = (ssm * inv * g_ref[...]).astype(BF16)


def _glu(yt, w_glu_bf, b_glu, ssm_out_g):
    w, t = yt.shape
    tn = GLU_TOKENS
    return pl.pallas_call(
        _glu_kernel,
        grid=(t // tn,),
        in_specs=[pl.BlockSpec((w, tn), lambda m: (0, m)),
                  pl.BlockSpec((w, w), lambda m: (0, 0)),
                  pl.BlockSpec((w, 1), lambda m: (0, 0)),
                  pl.BlockSpec((w, 1), lambda m: (0, 0))],
        out_specs=pl.BlockSpec((w, tn), lambda m: (0, m)),
        out_shape=jax.ShapeDtypeStruct((w, t), BF16),
        compiler_params=_params(1),
        name="glu",
    )(yt, w_glu_bf, b_glu.reshape(w, 1), ssm_out_g.reshape(w, 1))


OUT_COLS = 512


def _outproj_kernel(attn_ref, ssmt_ref, x_ref, mod_ref, ag_ref, wa_ref, ws_ref, g2_ref,
                    x1_ref, h2_ref):
    d = x_ref.shape[1]
    attn_n = (_rms(attn_ref[...].astype(F32)) * ag_ref[...]).astype(BF16)
    ssmt = ssmt_ref[...]
    sq = jnp.zeros((x_ref.shape[0], 1), F32)
    for c0 in range(0, d, OUT_COLS):
        cols = slice(c0, c0 + OUT_COLS)
        mixed = jnp.dot(attn_n, wa_ref[:, cols], preferred_element_type=F32)
        mixed = mixed + lax.dot_general(ssmt, ws_ref[:, cols], (((0,), (0,)), ((), ())),
                                        preferred_element_type=F32)
        x1 = x_ref[:, cols] + mod_ref[0, 2:3, cols] * mixed
        x1_ref[:, cols] = x1
        sq = sq + jnp.sum(x1 * x1, axis=-1, keepdims=True)
    inv = lax.rsqrt(sq * (1.0 / d) + EPS)
    for c0 in range(0, d, OUT_COLS):
        cols = slice(c0, c0 + OUT_COLS)
        gain = g2_ref[:, cols] * (1.0 + mod_ref[0, 4:5, cols])
        h2_ref[:, cols] = (x1_ref[:, cols] * inv * gain + mod_ref[0, 3:4, cols]).astype(BF16)


def _outproj(attn2, ssmt, x2, mod3, attn_out_g, w_out_bf, norm2_g, seq):
    t, d = x2.shape
    wa = attn2.shape[1]
    ws = ssmt.shape[0]
    tm = OUTPROJ_ROWS
    per_batch = seq // tm
    return pl.pallas_call(
        _outproj_kernel,
        grid=(t // tm,),
        in_specs=[pl.BlockSpec((tm, wa), lambda m: (m, 0)),
                  pl.BlockSpec((ws, tm), lambda m: (0, m)),
                  pl.BlockSpec((tm, d), lambda m: (m, 0)),
                  pl.BlockSpec((1, N_MOD, d), lambda m: (m // per_batch, 0, 0)),
                  pl.BlockSpec((1, wa), lambda m: (0, 0)),
                  pl.BlockSpec((wa, d), lambda m: (0, 0)),
                  pl.BlockSpec((ws, d), lambda m: (1, 0)),
                  pl.BlockSpec((1, d), lambda m: (0, 0))],
        out_specs=[pl.BlockSpec((tm, d), lambda m: (m, 0)),
                   pl.BlockSpec((tm, d), lambda m: (m, 0))],
        out_shape=[jax.ShapeDtypeStruct((t, d), F32),
                   jax.ShapeDtypeStruct((t, d), BF16)],
        compiler_params=_params(1),
        name="outproj",
    )(attn2, ssmt, x2, mod3, attn_out_g.reshape(1, wa), w_out_bf, w_out_bf,
      norm2_g.reshape(1, d))


def _ffn_kernel(h_ref, w1_ref, w2_ref, x1_ref, mod_ref, o_ref):
    f = pl.program_id(1)

    def partial_sum():
        a = jnp.dot(h_ref[...], w1_ref[...], preferred_element_type=F32)
        a = jnp.square(jnp.maximum(a, 0.0)).astype(BF16)
        return jnp.dot(a, w2_ref[...], preferred_element_type=F32)

    @pl.when(f == 0)
    def _():
        o_ref[...] = partial_sum()

    @pl.when(f > 0)
    def _():
        o_ref[...] += partial_sum()

    @pl.when(f == pl.num_programs(1) - 1)
    def _():
        o_ref[...] = x1_ref[...] + mod_ref[0, 5:6, :] * o_ref[...]


def _ffn(h2, w1_bf, w2_bf, x1, mod3, seq):
    t, d = x1.shape
    dff = w1_bf.shape[1]
    tm = FFN_ROWS
    tf = FFN_HIDDEN
    per_batch = seq // tm
    return pl.pallas_call(
        _ffn_kernel,
        grid=(t // tm, dff // tf),
        in_specs=[pl.BlockSpec((tm, d), lambda m, f: (m, 0)),
                  pl.BlockSpec((d, tf), lambda m, f: (0, f)),
                  pl.BlockSpec((tf, d), lambda m, f: (f, 0)),
                  pl.BlockSpec((tm, d), lambda m, f: (m, 0)),
                  pl.BlockSpec((1, N_MOD, d), lambda m, f: (m // per_batch, 0, 0))],
        out_specs=pl.BlockSpec((tm, d), lambda m, f: (m, 0)),
        out_shape=jax.ShapeDtypeStruct((t, d), F32),
        compiler_params=_params(2),
        name="ffn",
    )(h2, w1_bf, w2_bf, x1, mod3)


def _layer(x, c, w_ada, b_ada, norm1_g, w_in, q_norm_g, k_norm_g, lam_re, lam_im, log_step,
           b_re, b_im, c_re, c_im, d_skip, w_glu, b_glu, attn_out_g, ssm_out_g, w_out,
           norm2_g, w_ff1, w_ff2):
    bsz, seq, d = x.shape
    t = bsz * seq
    n_groups = lam_re.shape[0]
    chunks_per_seq = seq // CHUNK

    mod3 = _adaln(c, w_ada, b_ada).reshape(bsz, N_MOD, d)
    x2 = x.reshape(t, d)
    w_qkv = w_in[:, :3 * ATTN_WIDTH].astype(BF16)
    w_u = w_in[:, 3 * ATTN_WIDTH:].astype(BF16)
    qkv, u_g = _inproj(x2, mod3, norm1_g, w_qkv, w_u, q_norm_g, k_norm_g, seq)

    slopes = 2.0 ** (-8.0 * (jnp.arange(N_HEADS, dtype=F32) + 1.0) / N_HEADS)
    slopes = jnp.broadcast_to(slopes[:, None, None], (N_HEADS, 1, 2 * BAND))
    attn = _attention(qkv.reshape(bsz, seq, 3 * ATTN_WIDTH), slopes)

    krow, w_s, v_s, ap1, ap2 = _s5_params(lam_re, lam_im, log_step, b_re, b_im, c_re, c_im)
    d_rep = jnp.repeat(d_skip.reshape(n_groups, SSM_GROUP), CHUNK, axis=1)
    yt = _s5_main(u_g, krow, w_s, v_s, ap1, ap2, d_rep.reshape(n_groups, 1, SSM_GROUP * CHUNK),
                  chunks_per_seq)
    ssm = _glu(yt, w_glu.astype(BF16), b_glu, ssm_out_g)
    x1, h2 = _outproj(attn.reshape(t, ATTN_WIDTH), ssm, x2, mod3, attn_out_g,
                      w_out.astype(BF16), norm2_g, seq)
    out = _ffn(h2, w_ff1.astype(BF16), w_ff2.astype(BF16), x1, mod3, seq)
    return out.reshape(bsz, seq, d)


def kernel(x, c, w_ada, b_ada, norm1_g, w_in, q_norm_g, k_norm_g, lam_re, lam_im, log_step,
           b_re, b_im, c_re, c_im, d_skip, w_glu, b_glu, attn_out_g, ssm_out_g, w_out,
           norm2_g, w_ff1, w_ff2):
    for l in range(w_ada.shape[0]):
        x = _layer(x, c, w_ada[l], b_ada[l], norm1_g[l], w_in[l], q_norm_g[l], k_norm_g[l],
                   lam_re[l], lam_im[l], log_step[l], b_re[l], b_im[l], c_re[l], c_im[l],
                   d_skip[l], w_glu[l], b_glu[l], attn_out_g[l], ssm_out_g[l], w_out[l],
                   norm2_g[l], w_ff1[l], w_ff2[l])
    return x
```

```python
import functools
import math

import jax
import jax.numpy as jnp
from jax import lax
from jax.experimental import pallas as pl
from jax.experimental.pallas import tpu as pltpu

F32 = jnp.float32
BF16 = jnp.bfloat16

HEAD_DIM = 128
N_HEADS = 8
ATTN_WIDTH = N_HEADS * HEAD_DIM
SSM_GROUP = 16
STATE_DIM = 64
CHUNK = 128
BAND = 128
DILATIONS = (1, 4, 16)
N_MOD = 6
EPS = 1e-6
NEG = -1e30
LOG2E = math.log2(math.e)

V7X_VMEM_BYTES = 64 * 1024 * 1024
V7X_SUBLANES = 8
VMEM_LIMIT = V7X_VMEM_BYTES - 8 * 1024 * 1024

ADALN_COLS = 1024
INPROJ_ROWS = 1024
GLU_TOKENS = 1024
OUTPROJ_ROWS = 512
FFN_ROWS = 512
FFN_HIDDEN = 1024


def _params(n_axes):
    return pltpu.CompilerParams(
        dimension_semantics=("arbitrary",) * n_axes, vmem_limit_bytes=VMEM_LIMIT)


def _rms(x, eps=EPS):
    return x * lax.rsqrt(jnp.mean(x * x, axis=-1, keepdims=True) + eps)


def _adaln_kernel(c_ref, w_ref, b_ref, o_ref):
    c = c_ref[...]
    s = (c * jax.nn.sigmoid(c)).astype(BF16)
    o_ref[...] = jnp.dot(s, w_ref[...].astype(BF16), preferred_element_type=F32) + b_ref[...]


def _adaln(c, w_ada, b_ada):
    bsz, d = c.shape
    n_out = w_ada.shape[1]
    rows = V7X_SUBLANES
    tn = ADALN_COLS
    assert bsz <= rows
    c8 = jnp.zeros((rows, d), F32).at[:bsz].set(c)
    out = pl.pallas_call(
        _adaln_kernel,
        grid=(n_out // tn,),
        in_specs=[pl.BlockSpec((rows, d), lambda n: (0, 0)),
                  pl.BlockSpec((d, tn), lambda n: (0, n)),
                  pl.BlockSpec((1, tn), lambda n: (0, n))],
        out_specs=pl.BlockSpec((rows, tn), lambda n: (0, n)),
        out_shape=jax.ShapeDtypeStruct((rows, n_out), F32),
        compiler_params=_params(1),
        name="adaln",
    )(c8, w_ada, b_ada.reshape(1, n_out))
    return out[:bsz]


STAGE_PAD = 4


def _inproj_kernel(x_ref, mod_ref, g1_ref, w_ref, wu_ref, qg_ref, kg_ref, qkv_ref, u_ref,
                   h_ref, ustage):
    n = pl.program_id(1)
    tm = h_ref.shape[0]

    @pl.when(n == 0)
    def _():
        shift = mod_ref[0, 0:1, :]
        scale = mod_ref[0, 1:2, :]
        y = _rms(x_ref[...]) * g1_ref[...]
        h_ref[...] = (y * (1.0 + scale) + shift).astype(BF16)

    def head_norm(branch, gain):
        for pair in range(N_HEADS // 2):
            c0 = pair * 2 * HEAD_DIM
            w0 = branch * ATTN_WIDTH + c0
            acc = jnp.dot(h_ref[...], w_ref[:, w0:w0 + 2 * HEAD_DIM], preferred_element_type=F32)
            for hd in range(2):
                sl = slice(hd * HEAD_DIM, (hd + 1) * HEAD_DIM)
                qkv_ref[:, c0 + hd * HEAD_DIM:c0 + (hd + 1) * HEAD_DIM] = (
                    _rms(acc[:, sl]) * gain).astype(BF16)

    @pl.when(n == 0)
    def _():
        head_norm(0, qg_ref[...] * (HEAD_DIM ** -0.5 * LOG2E))

    @pl.when(n == 1)
    def _():
        head_norm(1, kg_ref[...])

    @pl.when(n == 2)
    def _():
        qkv_ref[...] = jnp.dot(h_ref[...], w_ref[:, 2 * ATTN_WIDTH:3 * ATTN_WIDTH],
                               preferred_element_type=F32).astype(BF16)

    @pl.when(n == 3)
    def _():
        width = wu_ref.shape[1]
        pitch = width + STAGE_PAD
        sub = 2 * CHUNK
        for c2 in range(tm // sub):
            ut = jnp.dot(h_ref[c2 * sub:(c2 + 1) * sub, :], wu_ref[...],
                         preferred_element_type=F32).T
            for cc in range(sub // CHUNK):
                c = c2 * (sub // CHUNK) + cc
                ustage[c * pitch:c * pitch + width, :] = ut[:, cc * CHUNK:(cc + 1) * CHUNK]

        def regroup(g, carry):
            for j in range(SSM_GROUP):
                u_ref[g, :, j * CHUNK:(j + 1) * CHUNK] = (
                    ustage[pl.ds(g * SSM_GROUP + j, tm // CHUNK, stride=pitch), :])
            return carry
        lax.fori_loop(0, width // SSM_GROUP, regroup, 0)


def _inproj(x2, mod3, norm1_g, w_qkv_bf, w_u_bf, q_norm_g, k_norm_g, seq):
    t, d = x2.shape
    tm = INPROJ_ROWS
    tn = ATTN_WIDTH
    width = w_u_bf.shape[1]
    n_groups = width // SSM_GROUP
    assert w_qkv_bf.shape[1] == 3 * tn and seq % tm == 0 and tm // CHUNK == V7X_SUBLANES
    per_batch = seq // tm
    return pl.pallas_call(
        _inproj_kernel,
        grid=(t // tm, 4),
        in_specs=[pl.BlockSpec((tm, d), lambda m, n: (m, 0)),
                  pl.BlockSpec((1, N_MOD, d), lambda m, n: (m // per_batch, 0, 0)),
                  pl.BlockSpec((1, d), lambda m, n: (0, 0)),
                  pl.BlockSpec((d, 3 * tn), lambda m, n: (0, 0), pipeline_mode=pl.Buffered(1)),
                  pl.BlockSpec((d, width), lambda m, n: (0, 0), pipeline_mode=pl.Buffered(1)),
                  pl.BlockSpec((1, HEAD_DIM), lambda m, n: (0, 0)),
                  pl.BlockSpec((1, HEAD_DIM), lambda m, n: (0, 0))],
        out_specs=[pl.BlockSpec((tm, tn), lambda m, n: (m, jnp.minimum(n, 2))),
                   pl.BlockSpec((n_groups, tm // CHUNK, SSM_GROUP * CHUNK), lambda m, n: (0, m, 0))],
        out_shape=[jax.ShapeDtypeStruct((t, 3 * tn), BF16),
                   jax.ShapeDtypeStruct((n_groups, t // CHUNK, SSM_GROUP * CHUNK), F32)],
        scratch_shapes=[pltpu.VMEM((tm, d), BF16),
                        pltpu.VMEM((tm // CHUNK * (width + STAGE_PAD), CHUNK), F32)],
        compiler_params=_params(2),
        name="inproj",
    )(x2, mod3, norm1_g.reshape(1, d), w_qkv_bf, w_u_bf, q_norm_g.reshape(1, HEAD_DIM),
      k_norm_g.reshape(1, HEAD_DIM))


ATTN_SUPER = BAND * max(DILATIONS)
BLOCKS_PER_SUPER = ATTN_SUPER // BAND
WIDE_DILATION = 16
WIDE_PITCH = BAND + 8


def _attn_kernel(slope_ref, q_ref, k_ref, v_ref, o_ref,
                 stage, stage4, q4, q16, k1, k4, k16, v1, v4, v16,
                 bias_ref, oacc, mstat, lstat, *, seq):
    assert DILATIONS == (1, 4, 16)
    n_pat = len(DILATIONS)
    n4 = ATTN_SUPER // 4
    n16 = ATTN_SUPER // 16

    def deinterleave(src, a1, a4, a16, pad):
        def chunk(c, carry):
            r0 = pl.multiple_of(c * ATTN_SUPER, ATTN_SUPER)
            x = src[pl.ds(r0, ATTN_SUPER), :]
            if a1 is not None:
                a1[0, pl.ds(pad + r0, ATTN_SUPER), :] = x
            stage[...] = x.astype(F32)
            for r in range(4):
                y = stage[pl.ds(r, n4, stride=4), :]
                stage4[r * n4:(r + 1) * n4, :] = y
                a4[r, pl.ds(pad + pl.multiple_of(c * n4, n4), n4), :] = y.astype(BF16)
            for r in range(4):
                for a in range(4):
                    z = stage4[pl.ds(r * n4 + a, n16, stride=4), :]
                    a16[4 * a + r, pl.ds(pad + pl.multiple_of(c * n16, n16), n16), :] = (
                        z.astype(BF16))
            return carry
        lax.fori_loop(0, seq // ATTN_SUPER, chunk, 0)

    for buf in (k1, k4, k16, v1, v4, v16):
        buf[:, 0:BAND, :] = jnp.zeros((buf.shape[0], BAND, HEAD_DIM), BF16)
    deinterleave(q_ref, None, q4, q16, 0)
    deinterleave(k_ref, k1, k4, k16, BAND)
    deinterleave(v_ref, v1, v4, v16, BAND)

    qi = lax.broadcasted_iota(jnp.int32, (BAND, 2 * BAND), 0)
    ki = lax.broadcasted_iota(jnp.int32, (BAND, 2 * BAND), 1)
    steps = qi - ki + BAND
    valid = (steps >= 0) & (steps <= BAND)
    slope = slope_ref[...]
    for p, d in enumerate(DILATIONS):
        bias = jnp.where(valid, -(slope * (float(d) * LOG2E)) * steps.astype(F32), NEG)
        bias_ref[2 * p + 1] = bias
        bias_ref[2 * p] = jnp.where(ki >= BAND, bias, NEG)

    q_streams = (None, q4, q16)
    k_streams = (k1, k4, k16)
    v_streams = (v1, v4, v16)
    ones = jnp.ones((2 * BAND, HEAD_DIM), BF16)

    def rows(start, size, d):
        return pl.ds(start, size) if d == 1 else pl.ds(start, size, stride=d)

    def block(p, d, r, j, local):
        row0 = pl.multiple_of(j * BAND, BAND)
        if d == 1:
            q = q_ref[pl.ds(row0, BAND), :]
        else:
            q = q_streams[p][r, pl.ds(row0, BAND), :]
        k2 = k_streams[p][r, pl.ds(row0, 2 * BAND), :]
        v2 = v_streams[p][r, pl.ds(row0, 2 * BAND), :]
        s = lax.dot_general(q, k2, (((1,), (1,)), ((), ())), preferred_element_type=F32)
        has_prev = jnp.minimum(j, 1)
        s = s + bias_ref[2 * p + has_prev]
        m = jnp.max(s, axis=-1, keepdims=True)
        e = jnp.exp2(s - m).astype(BF16)
        acc = jnp.dot(e, jnp.concatenate([v2, ones], axis=1), preferred_element_type=F32)
        if d == WIDE_DILATION:
            dst = pl.ds(pl.multiple_of(r * WIDE_PITCH, 8), BAND)
        else:
            dst = rows(local, BAND, d)
        oacc[p, dst, :] = acc[:, :HEAD_DIM]
        lstat[p, dst, :] = acc[:, HEAD_DIM:]
        mstat[p, dst, :] = jnp.broadcast_to(m, (BAND, HEAD_DIM))

    merge_rows = 256

    def super_block(sb, carry):
        p0 = sb * ATTN_SUPER

        def blocks(idx, c2):
            for p, d in enumerate(DILATIONS):
                r = idx & (d - 1)
                jl = idx >> int(math.log2(d))
                block(p, d, r, sb * (BLOCKS_PER_SUPER // d) + jl, r + BAND * d * jl)
            return c2
        lax.fori_loop(0, BLOCKS_PER_SUPER, blocks, 0, unroll=True)

        def merge(c, c2):
            r = pl.multiple_of(c * merge_rows, merge_rows)

            def natural(buf, p):
                if DILATIONS[p] != WIDE_DILATION:
                    return buf[p, pl.ds(r, merge_rows), :]
                i0 = c * (merge_rows // WIDE_DILATION)
                return jnp.concatenate(
                    [buf[p, pl.ds(i0 + i + half * 8 * WIDE_PITCH, 8, stride=WIDE_PITCH), :]
                     for i in range(merge_rows // WIDE_DILATION)
                     for half in range(WIDE_DILATION // 8)], axis=0)

            ms = [natural(mstat, p) for p in range(n_pat)]
            mx = functools.reduce(jnp.maximum, ms)
            num = jnp.zeros((merge_rows, HEAD_DIM), F32)
            den = jnp.zeros((merge_rows, HEAD_DIM), F32)
            for p in range(n_pat):
                w = jnp.exp2(ms[p] - mx)
                num = num + w * natural(oacc, p)
                den = den + w * natural(lstat, p)
            o_ref[pl.ds(pl.multiple_of(p0 + r, merge_rows), merge_rows), :] = (num / den).astype(BF16)
            return c2
        lax.fori_loop(0, ATTN_SUPER // merge_rows, merge, 0)
        return carry
    lax.fori_loop(0, seq // ATTN_SUPER, super_block, 0)


def _attention(qkv3, slopes):
    bsz, seq, _ = qkv3.shape
    assert seq % ATTN_SUPER == 0
    blk = lambda off: pl.BlockSpec((None, seq, HEAD_DIM), lambda b, h: (b, 0, off + h))
    stream = lambda d, pad: pltpu.VMEM((d, pad + seq // d, HEAD_DIM), BF16)
    stat = pltpu.VMEM((len(DILATIONS), WIDE_DILATION * WIDE_PITCH, HEAD_DIM), F32)
    return pl.pallas_call(
        functools.partial(_attn_kernel, seq=seq),
        grid=(bsz, N_HEADS),
        in_specs=[pl.BlockSpec((None, 1, 2 * BAND), lambda b, h: (h, 0, 0)),
                  blk(0), blk(N_HEADS), blk(2 * N_HEADS)],
        out_specs=pl.BlockSpec((None, seq, HEAD_DIM), lambda b, h: (b, 0, h)),
        out_shape=jax.ShapeDtypeStruct((bsz, seq, ATTN_WIDTH), BF16),
        scratch_shapes=[pltpu.VMEM((ATTN_SUPER, HEAD_DIM), F32),
                        pltpu.VMEM((ATTN_SUPER, HEAD_DIM), F32),
                        stream(4, 0), stream(16, 0),
                        stream(1, BAND), stream(4, BAND), stream(16, BAND),
                        stream(1, BAND), stream(4, BAND), stream(16, BAND),
                        pltpu.VMEM((2 * len(DILATIONS), BAND, 2 * BAND), F32),
                        stat, stat, stat],
        compiler_params=_params(2),
        name="attention",
    )(slopes, qkv3, qkv3, qkv3)


N_DOUBLINGS = 8


def _cmul(ar, ai, br, bi):
    return ar * br - ai * bi, ar * bi + ai * br


def _cpow(ar, ai, e, nbits):
    shape = jnp.broadcast_shapes(ar.shape, e.shape)
    pr = jnp.ones(shape, F32)
    pi = jnp.zeros(shape, F32)
    br, bi = ar, ai
    for k in range(nbits):
        nr, ni = _cmul(pr, pi, br, bi)
        sel = ((e >> k) & 1) == 1
        pr = jnp.where(sel, nr, pr)
        pi = jnp.where(sel, ni, pi)
        br, bi = _cmul(br, bi, br, bi)
    return pr, pi


def _zoh(lr, li, log_step):
    dt = jnp.exp(log_step)
    mag = jnp.exp(lr * dt)
    ar = mag * jnp.cos(li * dt)
    ai = mag * jnp.sin(li * dt)
    den = lr * lr + li * li
    cr = ((ar - 1.0) * lr + ai * li) / den
    ci = (ai * lr - (ar - 1.0) * li) / den
    return ar, ai, cr, ci


def _s5_params_kernel(lr_row, li_row, ls_row, lr_col, li_col, ls_col,
                      bt_re, bt_im, c_re, c_im, ct_re, ct_im,
                      krow_ref, w_ref, v_ref, ap1_ref, ap2_ref):
    p_dim = STATE_DIM
    ar, ai, cr, ci = _zoh(lr_row[...], li_row[...], ls_row[...])
    bbr, bbi = _cmul(cr, ci, bt_re[...], bt_im[...])
    s_idx = lax.broadcasted_iota(jnp.int32, (CHUNK, p_dim), 0)
    qr, qi = _cpow(ar, ai, CHUNK - 1 - s_idx, 7)
    for j in range(SSM_GROUP):
        wr, wi = _cmul(qr, qi, bbr[j:j + 1, :], bbi[j:j + 1, :])
        w_ref[j * CHUNK:(j + 1) * CHUNK, :] = jnp.concatenate([wr, wi], axis=-1).astype(BF16)
    mr, mi = ar, ai
    for _ in range(7):
        mr, mi = _cmul(mr, mi, mr, mi)
    for k in range(N_DOUBLINGS):
        ap1_ref[k:k + 1, :] = jnp.concatenate([mr, mr], axis=-1)
        ap2_ref[k:k + 1, :] = jnp.concatenate([-mi, mi], axis=-1)
        mr, mi = _cmul(mr, mi, mr, mi)

    acr, aci, _, _ = _zoh(lr_col[...], li_col[...], ls_col[...])
    t_idx = lax.broadcasted_iota(jnp.int32, (p_dim, CHUNK), 1)
    pr, pi = _cpow(acr, aci, t_idx, 7)
    cbr, cbi = [], []
    for j in range(SSM_GROUP):
        r_, i_ = _cmul(c_re[...], c_im[...], bbr[j:j + 1, :], bbi[j:j + 1, :])
        cbr.append(r_)
        cbi.append(i_)
    cbr = jnp.concatenate(cbr, axis=0)
    cbi = jnp.concatenate(cbi, axis=0)
    hi = lax.Precision.HIGHEST
    krow_ref[...] = (jnp.dot(cbr, pr, precision=hi, preferred_element_type=F32)
                     - jnp.dot(cbi, pi, precision=hi, preferred_element_type=F32))
    p1r, p1i = _cmul(pr, pi, acr, aci)
    ctr = ct_re[...]
    cti = ct_im[...]
    for i in range(SSM_GROUP):
        vr, vi = _cmul(ctr[:, i:i + 1], cti[:, i:i + 1], p1r, p1i)
        v_ref[0:p_dim, i * CHUNK:(i + 1) * CHUNK] = vr.astype(BF16)
        v_ref[p_dim:2 * p_dim, i * CHUNK:(i + 1) * CHUNK] = (-vi).astype(BF16)


def _s5_params(lam_re, lam_im, log_step, b_re, b_im, c_re, c_im):
    g, p = lam_re.shape
    n = SSM_GROUP
    row = lambda a: a.reshape(g, 1, p)
    col = lambda a: a.reshape(g, p, 1)
    ls = jnp.broadcast_to(log_step[:, None], (g, p))
    tr = lambda a: jnp.swapaxes(a, 1, 2)
    spec = lambda *shape: pl.BlockSpec((None,) + shape, lambda i: (i,) + (0,) * len(shape))
    return pl.pallas_call(
        _s5_params_kernel,
        grid=(g,),
        in_specs=[spec(1, p)] * 3 + [spec(p, 1)] * 3 + [spec(n, p)] * 4 + [spec(p, n)] * 2,
        out_specs=[spec(n * n, CHUNK), spec(n * CHUNK, 2 * p), spec(2 * p, n * CHUNK),
                   spec(N_DOUBLINGS, 2 * p), spec(N_DOUBLINGS, 2 * p)],
        out_shape=[jax.ShapeDtypeStruct((g, n * n, CHUNK), F32),
                   jax.ShapeDtypeStruct((g, n * CHUNK, 2 * p), BF16),
                   jax.ShapeDtypeStruct((g, 2 * p, n * CHUNK), BF16),
                   jax.ShapeDtypeStruct((g, N_DOUBLINGS, 2 * p), F32),
                   jax.ShapeDtypeStruct((g, N_DOUBLINGS, 2 * p), F32)],
        compiler_params=_params(1),
        name="s5_params",
    )(row(lam_re), row(lam_im), row(ls), col(lam_re), col(lam_im), col(ls),
      tr(b_re), tr(b_im), c_re, c_im, tr(c_re), tr(c_im))


GROUPS_PER_STEP = 2
T_COLS = 4 * CHUNK


def _s5_kernel(u_ref, krow_ref, knext_ref, w_ref, v_ref, ap1_ref, ap2_ref, d_ref, yt_ref,
               t_ref, ystage, *, chunks_per_seq):
    n = SSM_GROUP
    n_chunks = u_ref.shape[1]
    n_slabs = n * CHUNK // T_COLS
    per_slab = T_COLS // CHUNK

    s_idx = lax.broadcasted_iota(jnp.int32, (CHUNK, CHUNK), 0)
    t_idx = lax.broadcasted_iota(jnp.int32, (CHUNK, CHUNK), 1)
    causal = t_idx >= s_idx
    pos = lax.broadcasted_iota(jnp.int32, (n_chunks, 2 * STATE_DIM), 0) & (chunks_per_seq - 1)

    def build_slab(taps_ref, slot, c):
        def rows_j(j, carry):
            for e in range(per_slab):
                row = j * n + c * per_slab + e
                taps = jnp.broadcast_to(taps_ref[pl.ds(row, 1), :], (CHUNK, CHUNK))
                shifted = pltpu.roll(taps, 0, 1, stride=1, stride_axis=0)
                t_ref[slot, c, pl.ds(pl.multiple_of(j * CHUNK, CHUNK), CHUNK),
                      e * CHUNK:(e + 1) * CHUNK] = jnp.where(causal, shifted, 0.0).astype(BF16)
            return carry
        lax.fori_loop(0, n, rows_j, 0, unroll=True)

    def shift_rows(x, k):
        return jnp.where(pos >= k, pltpu.roll(x, k, 0), 0.0)

    def group(q, next_taps_ref):
        u = u_ref[q]
        ub = u.astype(BF16)
        b = jnp.dot(ub, w_ref[q], preferred_element_type=F32)
        h = shift_rows(b, 1)
        for k in range(int(math.log2(chunks_per_seq))):
            hs = shift_rows(h, 1 << k)
            h = (h + hs * ap1_ref[q, k:k + 1, :]
                 + pltpu.roll(hs, STATE_DIM, 1) * ap2_ref[q, k:k + 1, :])
        hv = jnp.dot(h.astype(BF16), v_ref[q], preferred_element_type=F32)

        def rebuild(c, carry):
            build_slab(next_taps_ref, 1 - q, c)
            return carry
        lax.fori_loop(0, n_slabs, rebuild, 0, unroll=True)

        pitch = n_chunks + STAGE_PAD
        base = q * n * pitch
        for c in range(n_slabs):
            yc = jnp.dot(ub, t_ref[q, c], preferred_element_type=F32)
            for e in range(per_slab):
                i = c * per_slab + e
                cols = slice(i * CHUNK, (i + 1) * CHUNK)
                yi = yc[:, e * CHUNK:(e + 1) * CHUNK] + hv[:, cols] + d_ref[q, :, cols] * u[:, cols]
                ystage[base + i * pitch:base + i * pitch + n_chunks, :] = jax.nn.gelu(yi)
        for c in range(n_chunks):
            for i0 in range(0, n, 8):
                yt_ref[q * n + i0:q * n + i0 + 8, c * CHUNK:(c + 1) * CHUNK] = (
                    ystage[pl.ds(base + i0 * pitch + c, 8, stride=pitch), :])

    @pl.when(pl.program_id(0) == 0)
    def _():
        def first(c, carry):
            build_slab(krow_ref.at[0], 0, c)
            return carry
        lax.fori_loop(0, n_slabs, first, 0)

    group(0, krow_ref.at[1])
    group(1, knext_ref)


def _s5_main(u_g, krow, w, v, ap1, ap2, d_rep, chunks_per_seq):
    g, n_chunks, width = u_g.shape
    gs = GROUPS_PER_STEP
    assert chunks_per_seq & (chunks_per_seq - 1) == 0 and chunks_per_seq <= 1 << N_DOUBLINGS
    assert g % gs == 0 and gs == 2
    spec = lambda *shape: pl.BlockSpec((gs,) + shape, lambda i: (i,) + (0,) * len(shape))
    taps = (SSM_GROUP * SSM_GROUP, CHUNK)
    return pl.pallas_call(
        functools.partial(_s5_kernel, chunks_per_seq=chunks_per_seq),
        grid=(g // gs,),
        in_specs=[spec(n_chunks, width), spec(*taps),
                  pl.BlockSpec((None,) + taps, lambda i: (jnp.minimum(gs * i + gs, g - 1), 0, 0)),
                  spec(width, 2 * STATE_DIM), spec(2 * STATE_DIM, width),
                  spec(N_DOUBLINGS, 2 * STATE_DIM), spec(N_DOUBLINGS, 2 * STATE_DIM),
                  spec(1, width)],
        out_specs=pl.BlockSpec((gs * SSM_GROUP, n_chunks * CHUNK), lambda i: (i, 0)),
        out_shape=jax.ShapeDtypeStruct((g * SSM_GROUP, n_chunks * CHUNK), F32),
        scratch_shapes=[pltpu.VMEM((gs, width // T_COLS, width, T_COLS), BF16),
                        pltpu.VMEM((gs * SSM_GROUP * (n_chunks + STAGE_PAD), CHUNK), F32)],
        compiler_params=_params(1),
        name="s5_main",
    )(u_g, krow, krow, w, v, ap1, ap2, d_rep)


def _glu_kernel(yt_ref, w_ref, b_ref, g_ref, o_ref):
    y = yt_ref[...]
    z = lax.dot_general(w_ref[...], y.astype(BF16), (((0,), (0,)), ((), ())),
                        preferred_element_type=F32) + b_ref[...]
    ssm = y * jax.nn.sigmoid(z)
    inv = lax.rsqrt(jnp.mean(ssm * ssm, axis=0, keepdims=True) + EPS)
    o_ref[...] = (ssm * inv * g_ref[...]).astype(BF16)


def _glu(yt, w_glu_bf, b_glu, ssm_out_g):
    w, t = yt.shape
    tn = GLU_TOKENS
    return pl.pallas_call(
        _glu_kernel,
        grid=(t // tn,),
        in_specs=[pl.BlockSpec((w, tn), lambda m: (0, m)),
                  pl.BlockSpec((w, w), lambda m: (0, 0)),
                  pl.BlockSpec((w, 1), lambda m: (0, 0)),
                  pl.BlockSpec((w, 1), lambda m: (0, 0))],
        out_specs=pl.BlockSpec((w, tn), lambda m: (0, m)),
        out_shape=jax.ShapeDtypeStruct((w, t), BF16),
        compiler_params=_params(1),
        name="glu",
    )(yt, w_glu_bf, b_glu.reshape(w, 1), ssm_out_g.reshape(w, 1))


OUT_COLS = 512


def _outproj_kernel(attn_ref, ssmt_ref, x_ref, mod_ref, ag_ref, wa_ref, ws_ref, g2_ref,
                    x1_ref, h2_ref):
    d = x_ref.shape[1]
    attn_n = (_rms(attn_ref[...].astype(F32)) * ag_ref[...]).astype(BF16)
    ssmt = ssmt_ref[...]
    sq = jnp.zeros((x_ref.shape[0], 1), F32)
    for c0 in range(0, d, OUT_COLS):
        cols = slice(c0, c0 + OUT_COLS)
        mixed = jnp.dot(attn_n, wa_ref[:, cols], preferred_element_type=F32)
        mixed = mixed + lax.dot_general(ssmt, ws_ref[:, cols], (((0,), (0,)), ((), ())),
                                        preferred_element_type=F32)
        x1 = x_ref[:, cols] + mod_ref[0, 2:3, cols] * mixed
        x1_ref[:, cols] = x1
        sq = sq + jnp.sum(x1 * x1, axis=-1, keepdims=True)
    inv = lax.rsqrt(sq * (1.0 / d) + EPS)
    for c0 in range(0, d, OUT_COLS):
        cols = slice(c0, c0 + OUT_COLS)
        gain = g2_ref[:, cols] * (1.0 + mod_ref[0, 4:5, cols])
        h2_ref[:, cols] = (x1_ref[:, cols] * inv * gain + mod_ref[0, 3:4, cols]).astype(BF16)


def _outproj(attn2, ssmt, x2, mod3, attn_out_g, w_out_bf, norm2_g, seq):
    t, d = x2.shape
    wa = attn2.shape[1]
    ws = ssmt.shape[0]
    tm = OUTPROJ_ROWS
    per_batch = seq // tm
    return pl.pallas_call(
        _outproj_kernel,
        grid=(t // tm,),
        in_specs=[pl.BlockSpec((tm, wa), lambda m: (m, 0)),
                  pl.BlockSpec((ws, tm), lambda m: (0, m)),
                  pl.BlockSpec((tm, d), lambda m: (m, 0)),
                  pl.BlockSpec((1, N_MOD, d), lambda m: (m // per_batch, 0, 0)),
                  pl.BlockSpec((1, wa), lambda m: (0, 0)),
                  pl.BlockSpec((wa, d), lambda m: (0, 0)),
                  pl.BlockSpec((ws, d), lambda m: (1, 0)),
                  pl.BlockSpec((1, d), lambda m: (0, 0))],
        out_specs=[pl.BlockSpec((tm, d), lambda m: (m, 0)),
                   pl.BlockSpec((tm, d), lambda m: (m, 0))],
        out_shape=[jax.ShapeDtypeStruct((t, d), F32),
                   jax.ShapeDtypeStruct((t, d), BF16)],
        compiler_params=_params(1),
        name="outproj",
    )(attn2, ssmt, x2, mod3, attn_out_g.reshape(1, wa), w_out_bf, w_out_bf,
      norm2_g.reshape(1, d))


def _ffn_kernel(h_ref, w1_ref, w2_ref, x1_ref, mod_ref, o_ref):
    f = pl.program_id(1)

    def partial_sum():
        a = jnp.dot(h_ref[...], w1_ref[...], preferred_element_type=F32)
        a = jnp.square(jnp.maximum(a, 0.0)).astype(BF16)
        return jnp.dot(a, w2_ref[...], preferred_element_type=F32)

    @pl.when(f == 0)
    def _():
        o_ref[...] = partial_sum()

    @pl.when(f > 0)
    def _():
        o_ref[...] += partial_sum()

    @pl.when(f == pl.num_programs(1) - 1)
    def _():
        o_ref[...] = x1_ref[...] + mod_ref[0, 5:6, :] * o_ref[...]


def _ffn(h2, w1_bf, w2_bf, x1, mod3, seq):
    t, d = x1.shape
    dff = w1_bf.shape[1]
    tm = FFN_ROWS
    tf = FFN_HIDDEN
    per_batch = seq // tm
    return pl.pallas_call(
        _ffn_kernel,
        grid=(t // tm, dff // tf),
        in_specs=[pl.BlockSpec((tm, d), lambda m, f: (m, 0)),
                  pl.BlockSpec((d, tf), lambda m, f: (0, f)),
                  pl.BlockSpec((tf, d), lambda m, f: (f, 0)),
                  pl.BlockSpec((tm, d), lambda m, f: (m, 0)),
                  pl.BlockSpec((1, N_MOD, d), lambda m, f: (m // per_batch, 0, 0))],
        out_specs=pl.BlockSpec((tm, d), lambda m, f: (m, 0)),
        out_shape=jax.ShapeDtypeStruct((t, d), F32),
        compiler_params=_params(2),
        name="ffn",
    )(h2, w1_bf, w2_bf, x1, mod3)


def _layer(x, c, w_ada, b_ada, norm1_g, w_in, q_norm_g, k_norm_g, lam_re, lam_im, log_step,
           b_re, b_im, c_re, c_im, d_skip, w_glu, b_glu, attn_out_g, ssm_out_g, w_out,
           norm2_g, w_ff1, w_ff2):
    bsz, seq, d = x.shape
    t = bsz * seq
    n_groups = lam_re.shape[0]
    chunks_per_seq = seq // CHUNK

    mod3 = _adaln(c, w_ada, b_ada).reshape(bsz, N_MOD, d)
    x2 = x.reshape(t, d)
    w_qkv = w_in[:, :3 * ATTN_WIDTH].astype(BF16)
    w_u = w_in[:, 3 * ATTN_WIDTH:].astype(BF16)
    qkv, u_g = _inproj(x2, mod3, norm1_g, w_qkv, w_u, q_norm_g, k_norm_g, seq)

    slopes = 2.0 ** (-8.0 * (jnp.arange(N_HEADS, dtype=F32) + 1.0) / N_HEADS)
    slopes = jnp.broadcast_to(slopes[:, None, None], (N_HEADS, 1, 2 * BAND))
    attn = _attention(qkv.reshape(bsz, seq, 3 * ATTN_WIDTH), slopes)

    krow, w_s, v_s, ap1, ap2 = _s5_params(lam_re, lam_im, log_step, b_re, b_im, c_re, c_im)
    d_rep = jnp.repeat(d_skip.reshape(n_groups, SSM_GROUP), CHUNK, axis=1)
    yt = _s5_main(u_g, krow, w_s, v_s, ap1, ap2, d_rep.reshape(n_groups, 1, SSM_GROUP * CHUNK),
                  chunks_per_seq)
    ssm = _glu(yt, w_glu.astype(BF16), b_glu, ssm_out_g)
    x1, h2 = _outproj(attn.reshape(t, ATTN_WIDTH), ssm, x2, mod3, attn_out_g,
                      w_out.astype(BF16), norm2_g, seq)
    out = _ffn(h2, w_ff1.astype(BF16), w_ff2.astype(BF16), x1, mod3, seq)
    return out.reshape(bsz, seq, d)


def kernel(x, c, w_ada, b_ada, norm1_g, w_in, q_norm_g, k_norm_g, lam_re, lam_im, log_step,
           b_re, b_im, c_re, c_im, d_skip, w_glu, b_glu, attn_out_g, ssm_out_g, w_out,
           norm2_g, w_ff1, w_ff2):
    for l in range(w_ada.shape[0]):
        x = _layer(x, c, w_ada[l], b_ada[l], norm1_g[l], w_in[l], q_norm_g[l], k_norm_g[l],
                   lam_re[l], lam_im[l], log_step[l], b_re[l], b_im[l], c_re[l], c_im[l],
                   d_skip[l], w_glu[l], b_glu[l], attn_out_g[l], ssm_out_g[l], w_out[l],
                   norm2_g[l], w_ff1[l], w_ff2[l])
    return x
```

```python
import functools
import math

import jax
import jax.numpy as jnp
from jax import lax
from jax.experimental import pallas as pl
from jax.experimental.pallas import tpu as pltpu

F32 = jnp.float32
BF16 = jnp.bfloat16

HEAD_DIM = 128
N_HEADS = 8
ATTN_WIDTH = N_HEADS * HEAD_DIM
SSM_GROUP = 16
STATE_DIM = 64
CHUNK = 128
BAND = 128
DILATIONS = (1, 4, 16)
N_MOD = 6
EPS = 1e-6
NEG = -1e30
LOG2E = math.log2(math.e)

V7X_VMEM_BYTES = 64 * 1024 * 1024
V7X_SUBLANES = 8
VMEM_LIMIT = V7X_VMEM_BYTES - 8 * 1024 * 1024

ADALN_COLS = 1024
INPROJ_ROWS = 1024
GLU_TOKENS = 1024
OUTPROJ_ROWS = 512
FFN_ROWS = 512
FFN_HIDDEN = 1024


def _params(n_axes):
    return pltpu.CompilerParams(
        dimension_semantics=("arbitrary",) * n_axes, vmem_limit_bytes=VMEM_LIMIT)


def _rms(x, eps=EPS):
    return x * lax.rsqrt(jnp.mean(x * x, axis=-1, keepdims=True) + eps)


def _adaln_kernel(c_ref, w_ref, b_ref, o_ref):
    c = c_ref[...]
    s = (c * jax.nn.sigmoid(c)).astype(BF16)
    o_ref[...] = jnp.dot(s, w_ref[...].astype(BF16), preferred_element_type=F32) + b_ref[...]


def _adaln(c, w_ada, b_ada):
    bsz, d = c.shape
    n_out = w_ada.shape[1]
    rows = V7X_SUBLANES
    tn = ADALN_COLS
    assert bsz <= rows
    c8 = jnp.zeros((rows, d), F32).at[:bsz].set(c)
    out = pl.pallas_call(
        _adaln_kernel,
        grid=(n_out // tn,),
        in_specs=[pl.BlockSpec((rows, d), lambda n: (0, 0)),
                  pl.BlockSpec((d, tn), lambda n: (0, n)),
                  pl.BlockSpec((1, tn), lambda n: (0, n))],
        out_specs=pl.BlockSpec((rows, tn), lambda n: (0, n)),
        out_shape=jax.ShapeDtypeStruct((rows, n_out), F32),
        compiler_params=_params(1),
        name="adaln",
    )(c8, w_ada, b_ada.reshape(1, n_out))
    return out[:bsz]


STAGE_PAD = 4


def _inproj_kernel(x_ref, mod_ref, g1_ref, w_ref, wu_ref, qg_ref, kg_ref, qkv_ref, u_ref,
                   h_ref, ustage):
    n = pl.program_id(1)
    tm = h_ref.shape[0]

    @pl.when(n == 0)
    def _():
        shift = mod_ref[0, 0:1, :]
        scale = mod_ref[0, 1:2, :]
        y = _rms(x_ref[...]) * g1_ref[...]
        h_ref[...] = (y * (1.0 + scale) + shift).astype(BF16)

    def head_norm(branch, gain):
        for pair in range(N_HEADS // 2):
            c0 = pair * 2 * HEAD_DIM
            w0 = branch * ATTN_WIDTH + c0
            acc = jnp.dot(h_ref[...], w_ref[:, w0:w0 + 2 * HEAD_DIM], preferred_element_type=F32)
            for hd in range(2):
                sl = slice(hd * HEAD_DIM, (hd + 1) * HEAD_DIM)
                qkv_ref[2 * pair + hd] = (_rms(acc[:, sl]) * gain).astype(BF16)

    @pl.when(n == 0)
    def _():
        head_norm(0, qg_ref[...] * (HEAD_DIM ** -0.5 * LOG2E))

    @pl.when(n == 1)
    def _():
        head_norm(1, kg_ref[...])

    @pl.when(n == 2)
    def _():
        acc = jnp.dot(h_ref[...], w_ref[:, 2 * ATTN_WIDTH:3 * ATTN_WIDTH],
                      preferred_element_type=F32)
        for hd in range(N_HEADS):
            qkv_ref[hd] = acc[:, hd * HEAD_DIM:(hd + 1) * HEAD_DIM].astype(BF16)

    @pl.when(n == 3)
    def _():
        width = wu_ref.shape[1]
        pitch = width + STAGE_PAD
        sub = 2 * CHUNK
        for c2 in range(tm // sub):
            ut = jnp.dot(h_ref[c2 * sub:(c2 + 1) * sub, :], wu_ref[...],
                         preferred_element_type=F32).T
            for cc in range(sub // CHUNK):
                c = c2 * (sub // CHUNK) + cc
                ustage[c * pitch:c * pitch + width, :] = ut[:, cc * CHUNK:(cc + 1) * CHUNK]

        def regroup(g, carry):
            for j in range(SSM_GROUP):
                u_ref[g, :, j * CHUNK:(j + 1) * CHUNK] = (
                    ustage[pl.ds(g * SSM_GROUP + j, tm // CHUNK, stride=pitch), :])
            return carry
        lax.fori_loop(0, width // SSM_GROUP, regroup, 0)


def _inproj(x2, mod3, norm1_g, w_qkv_bf, w_u_bf, q_norm_g, k_norm_g, seq):
    t, d = x2.shape
    tm = INPROJ_ROWS
    tn = ATTN_WIDTH
    width = w_u_bf.shape[1]
    n_groups = width // SSM_GROUP
    assert w_qkv_bf.shape[1] == 3 * tn and seq % tm == 0 and tm // CHUNK == V7X_SUBLANES
    per_batch = seq // tm
    return pl.pallas_call(
        _inproj_kernel,
        grid=(t // tm, 4),
        in_specs=[pl.BlockSpec((tm, d), lambda m, n: (m, 0)),
                  pl.BlockSpec((1, N_MOD, d), lambda m, n: (m // per_batch, 0, 0)),
                  pl.BlockSpec((1, d), lambda m, n: (0, 0)),
                  pl.BlockSpec((d, 3 * tn), lambda m, n: (0, 0), pipeline_mode=pl.Buffered(1)),
                  pl.BlockSpec((d, width), lambda m, n: (0, 0), pipeline_mode=pl.Buffered(1)),
                  pl.BlockSpec((1, HEAD_DIM), lambda m, n: (0, 0)),
                  pl.BlockSpec((1, HEAD_DIM), lambda m, n: (0, 0))],
        out_specs=[pl.BlockSpec((N_HEADS, tm, HEAD_DIM), lambda m, n: (jnp.minimum(n, 2), m, 0)),
                   pl.BlockSpec((n_groups, tm // CHUNK, SSM_GROUP * CHUNK), lambda m, n: (0, m, 0))],
        out_shape=[jax.ShapeDtypeStruct((3 * N_HEADS, t, HEAD_DIM), BF16),
                   jax.ShapeDtypeStruct((n_groups, t // CHUNK, SSM_GROUP * CHUNK), F32)],
        scratch_shapes=[pltpu.VMEM((tm, d), BF16),
                        pltpu.VMEM((tm // CHUNK * (width + STAGE_PAD), CHUNK), F32)],
        compiler_params=_params(2),
        name="inproj",
    )(x2, mod3, norm1_g.reshape(1, d), w_qkv_bf, w_u_bf, q_norm_g.reshape(1, HEAD_DIM),
      k_norm_g.reshape(1, HEAD_DIM))


ATTN_SUPER = BAND * max(DILATIONS)
BLOCKS_PER_SUPER = ATTN_SUPER // BAND
WIDE_DILATION = 16
WIDE_PITCH = BAND + 8


def _attn_kernel(slope_ref, q_ref, k_ref, v_ref, o_ref,
                 stage, stage4, q4, q16, k1, k4, k16, v1, v4, v16,
                 bias_ref, oacc, mstat, lstat, *, seq):
    assert DILATIONS == (1, 4, 16)
    n_pat = len(DILATIONS)
    n4 = ATTN_SUPER // 4
    n16 = ATTN_SUPER // 16

    def deinterleave(src, a1, a4, a16, pad):
        def chunk(c, carry):
            r0 = pl.multiple_of(c * ATTN_SUPER, ATTN_SUPER)
            x = src[pl.ds(r0, ATTN_SUPER), :]
            if a1 is not None:
                a1[0, pl.ds(pad + r0, ATTN_SUPER), :] = x
            stage[...] = x.astype(F32)
            for r in range(4):
                y = stage[pl.ds(r, n4, stride=4), :]
                stage4[r * n4:(r + 1) * n4, :] = y
                a4[r, pl.ds(pad + pl.multiple_of(c * n4, n4), n4), :] = y.astype(BF16)
            for r in range(4):
                for a in range(4):
                    z = stage4[pl.ds(r * n4 + a, n16, stride=4), :]
                    a16[4 * a + r, pl.ds(pad + pl.multiple_of(c * n16, n16), n16), :] = (
                        z.astype(BF16))
            return carry
        lax.fori_loop(0, seq // ATTN_SUPER, chunk, 0)

    for buf in (k1, k4, k16, v1, v4, v16):
        buf[:, 0:BAND, :] = jnp.zeros((buf.shape[0], BAND, HEAD_DIM), BF16)
    deinterleave(q_ref, None, q4, q16, 0)
    deinterleave(k_ref, k1, k4, k16, BAND)
    deinterleave(v_ref, v1, v4, v16, BAND)

    qi = lax.broadcasted_iota(jnp.int32, (BAND, 2 * BAND), 0)
    ki = lax.broadcasted_iota(jnp.int32, (BAND, 2 * BAND), 1)
    steps = qi - ki + BAND
    valid = (steps >= 0) & (steps <= BAND)
    slope = slope_ref[...]
    for p, d in enumerate(DILATIONS):
        bias = jnp.where(valid, -(slope * (float(d) * LOG2E)) * steps.astype(F32), NEG)
        bias_ref[2 * p + 1] = bias
        bias_ref[2 * p] = jnp.where(ki >= BAND, bias, NEG)

    q_streams = (None, q4, q16)
    k_streams = (k1, k4, k16)
    v_streams = (v1, v4, v16)
    ones = jnp.ones((2 * BAND, HEAD_DIM), BF16)

    def rows(start, size, d):
        return pl.ds(start, size) if d == 1 else pl.ds(start, size, stride=d)

    def block(p, d, r, j, local):
        row0 = pl.multiple_of(j * BAND, BAND)
        if d == 1:
            q = q_ref[pl.ds(row0, BAND), :]
        else:
            q = q_streams[p][r, pl.ds(row0, BAND), :]
        k2 = k_streams[p][r, pl.ds(row0, 2 * BAND), :]
        v2 = v_streams[p][r, pl.ds(row0, 2 * BAND), :]
        s = lax.dot_general(q, k2, (((1,), (1,)), ((), ())), preferred_element_type=F32)
        has_prev = jnp.minimum(j, 1)
        s = s + bias_ref[2 * p + has_prev]
        m = jnp.max(s, axis=-1, keepdims=True)
        e = jnp.exp2(s - m).astype(BF16)
        acc = jnp.dot(e, jnp.concatenate([v2, ones], axis=1), preferred_element_type=F32)
        if d == WIDE_DILATION:
            dst = pl.ds(pl.multiple_of(r * WIDE_PITCH, 8), BAND)
        else:
            dst = rows(local, BAND, d)
        oacc[p, dst, :] = acc[:, :HEAD_DIM]
        lstat[p, dst, :] = acc[:, HEAD_DIM:]
        mstat[p, dst, :] = jnp.broadcast_to(m, (BAND, HEAD_DIM))

    merge_rows = 256

    def super_block(sb, carry):
        p0 = sb * ATTN_SUPER

        def blocks(idx, c2):
            for p, d in enumerate(DILATIONS):
                r = idx & (d - 1)
                jl = idx >> int(math.log2(d))
                block(p, d, r, sb * (BLOCKS_PER_SUPER // d) + jl, r + BAND * d * jl)
            return c2
        lax.fori_loop(0, BLOCKS_PER_SUPER, blocks, 0, unroll=True)

        def merge(c, c2):
            r = pl.multiple_of(c * merge_rows, merge_rows)

            def natural(buf, p):
                if DILATIONS[p] != WIDE_DILATION:
                    return buf[p, pl.ds(r, merge_rows), :]
                i0 = c * (merge_rows // WIDE_DILATION)
                return jnp.concatenate(
                    [buf[p, pl.ds(i0 + i + half * 8 * WIDE_PITCH, 8, stride=WIDE_PITCH), :]
                     for i in range(merge_rows // WIDE_DILATION)
                     for half in range(WIDE_DILATION // 8)], axis=0)

            ms = [natural(mstat, p) for p in range(n_pat)]
            mx = functools.reduce(jnp.maximum, ms)
            num = jnp.zeros((merge_rows, HEAD_DIM), F32)
            den = jnp.zeros((merge_rows, HEAD_DIM), F32)
            for p in range(n_pat):
                w = jnp.exp2(ms[p] - mx)
                num = num + w * natural(oacc, p)
                den = den + w * natural(lstat, p)
            o_ref[pl.ds(pl.multiple_of(p0 + r, merge_rows), merge_rows), :] = (num / den).astype(BF16)
            return c2
        lax.fori_loop(0, ATTN_SUPER // merge_rows, merge, 0)
        return carry
    lax.fori_loop(0, seq // ATTN_SUPER, super_block, 0)


def _attention(qkv4, slopes):
    _, bsz, seq, _ = qkv4.shape
    assert seq % ATTN_SUPER == 0
    blk = lambda off: pl.BlockSpec((None, None, seq, HEAD_DIM), lambda b, h: (off + h, b, 0, 0))
    stream = lambda d, pad: pltpu.VMEM((d, pad + seq // d, HEAD_DIM), BF16)
    stat = pltpu.VMEM((len(DILATIONS), WIDE_DILATION * WIDE_PITCH, HEAD_DIM), F32)
    return pl.pallas_call(
        functools.partial(_attn_kernel, seq=seq),
        grid=(bsz, N_HEADS),
        in_specs=[pl.BlockSpec((None, 1, 2 * BAND), lambda b, h: (h, 0, 0)),
                  blk(0), blk(N_HEADS), blk(2 * N_HEADS)],
        out_specs=pl.BlockSpec((None, None, seq, HEAD_DIM), lambda b, h: (h, b, 0, 0)),
        out_shape=jax.ShapeDtypeStruct((N_HEADS, bsz, seq, HEAD_DIM), BF16),
        scratch_shapes=[pltpu.VMEM((ATTN_SUPER, HEAD_DIM), F32),
                        pltpu.VMEM((ATTN_SUPER, HEAD_DIM), F32),
                        stream(4, 0), stream(16, 0),
                        stream(1, BAND), stream(4, BAND), stream(16, BAND),
                        stream(1, BAND), stream(4, BAND), stream(16, BAND),
                        pltpu.VMEM((2 * len(DILATIONS), BAND, 2 * BAND), F32),
                        stat, stat, stat],
        compiler_params=_params(2),
        name="attention",
    )(slopes, qkv4, qkv4, qkv4)


N_DOUBLINGS = 8


def _cmul(ar, ai, br, bi):
    return ar * br - ai * bi, ar * bi + ai * br


def _cpow(ar, ai, e, nbits):
    shape = jnp.broadcast_shapes(ar.shape, e.shape)
    pr = jnp.ones(shape, F32)
    pi = jnp.zeros(shape, F32)
    br, bi = ar, ai
    for k in range(nbits):
        nr, ni = _cmul(pr, pi, br, bi)
        sel = ((e >> k) & 1) == 1
        pr = jnp.where(sel, nr, pr)
        pi = jnp.where(sel, ni, pi)
        br, bi = _cmul(br, bi, br, bi)
    return pr, pi


def _zoh(lr, li, log_step):
    dt = jnp.exp(log_step)
    mag = jnp.exp(lr * dt)
    ar = mag * jnp.cos(li * dt)
    ai = mag * jnp.sin(li * dt)
    den = lr * lr + li * li
    cr = ((ar - 1.0) * lr + ai * li) / den
    ci = (ai * lr - (ar - 1.0) * li) / den
    return ar, ai, cr, ci


def _s5_params_kernel(lr_row, li_row, ls_row, lr_col, li_col, ls_col,
                      bt_re, bt_im, c_re, c_im, ct_re, ct_im,
                      krow_ref, w_ref, v_ref, ap1_ref, ap2_ref):
    p_dim = STATE_DIM
    ar, ai, cr, ci = _zoh(lr_row[...], li_row[...], ls_row[...])
    bbr, bbi = _cmul(cr, ci, bt_re[...], bt_im[...])
    s_idx = lax.broadcasted_iota(jnp.int32, (CHUNK, p_dim), 0)
    qr, qi = _cpow(ar, ai, CHUNK - 1 - s_idx, 7)
    for j in range(SSM_GROUP):
        wr, wi = _cmul(qr, qi, bbr[j:j + 1, :], bbi[j:j + 1, :])
        w_ref[j * CHUNK:(j + 1) * CHUNK, :] = jnp.concatenate([wr, wi], axis=-1).astype(BF16)
    mr, mi = ar, ai
    for _ in range(7):
        mr, mi = _cmul(mr, mi, mr, mi)
    for k in range(N_DOUBLINGS):
        ap1_ref[k:k + 1, :] = jnp.concatenate([mr, mr], axis=-1)
        ap2_ref[k:k + 1, :] = jnp.concatenate([-mi, mi], axis=-1)
        mr, mi = _cmul(mr, mi, mr, mi)

    acr, aci, _, _ = _zoh(lr_col[...], li_col[...], ls_col[...])
    t_idx = lax.broadcasted_iota(jnp.int32, (p_dim, CHUNK), 1)
    pr, pi = _cpow(acr, aci, t_idx, 7)
    cbr, cbi = [], []
    for j in range(SSM_GROUP):
        r_, i_ = _cmul(c_re[...], c_im[...], bbr[j:j + 1, :], bbi[j:j + 1, :])
        cbr.append(r_)
        cbi.append(i_)
    cbr = jnp.concatenate(cbr, axis=0)
    cbi = jnp.concatenate(cbi, axis=0)
    hi = lax.Precision.HIGHEST
    krow_ref[...] = (jnp.dot(cbr, pr, precision=hi, preferred_element_type=F32)
                     - jnp.dot(cbi, pi, precision=hi, preferred_element_type=F32))
    p1r, p1i = _cmul(pr, pi, acr, aci)
    ctr = ct_re[...]
    cti = ct_im[...]
    for i in range(SSM_GROUP):
        vr, vi = _cmul(ctr[:, i:i + 1], cti[:, i:i + 1], p1r, p1i)
        v_ref[0:p_dim, i * CHUNK:(i + 1) * CHUNK] = vr.astype(BF16)
        v_ref[p_dim:2 * p_dim, i * CHUNK:(i + 1) * CHUNK] = (-vi).astype(BF16)


def _s5_params(lam_re, lam_im, log_step, b_re, b_im, c_re, c_im):
    g, p = lam_re.shape
    n = SSM_GROUP
    row = lambda a: a.reshape(g, 1, p)
    col = lambda a: a.reshape(g, p, 1)
    ls = jnp.broadcast_to(log_step[:, None], (g, p))
    tr = lambda a: jnp.swapaxes(a, 1, 2)
    spec = lambda *shape: pl.BlockSpec((None,) + shape, lambda i: (i,) + (0,) * len(shape))
    return pl.pallas_call(
        _s5_params_kernel,
        grid=(g,),
        in_specs=[spec(1, p)] * 3 + [spec(p, 1)] * 3 + [spec(n, p)] * 4 + [spec(p, n)] * 2,
        out_specs=[spec(n * n, CHUNK), spec(n * CHUNK, 2 * p), spec(2 * p, n * CHUNK),
                   spec(N_DOUBLINGS, 2 * p), spec(N_DOUBLINGS, 2 * p)],
        out_shape=[jax.ShapeDtypeStruct((g, n * n, CHUNK), F32),
                   jax.ShapeDtypeStruct((g, n * CHUNK, 2 * p), BF16),
                   jax.ShapeDtypeStruct((g, 2 * p, n * CHUNK), BF16),
                   jax.ShapeDtypeStruct((g, N_DOUBLINGS, 2 * p), F32),
                   jax.ShapeDtypeStruct((g, N_DOUBLINGS, 2 * p), F32)],
        compiler_params=_params(1),
        name="s5_params",
    )(row(lam_re), row(lam_im), row(ls), col(lam_re), col(lam_im), col(ls),
      tr(b_re), tr(b_im), c_re, c_im, tr(c_re), tr(c_im))


GROUPS_PER_STEP = 2
T_COLS = 4 * CHUNK


def _s5_kernel(u_ref, krow_ref, knext_ref, w_ref, v_ref, ap1_ref, ap2_ref, d_ref, yt_ref,
               t_ref, ystage, *, chunks_per_seq):
    n = SSM_GROUP
    n_chunks = u_ref.shape[1]
    n_slabs = n * CHUNK // T_COLS
    per_slab = T_COLS // CHUNK

    s_idx = lax.broadcasted_iota(jnp.int32, (CHUNK, CHUNK), 0)
    t_idx = lax.broadcasted_iota(jnp.int32, (CHUNK, CHUNK), 1)
    causal = t_idx >= s_idx
    pos = lax.broadcasted_iota(jnp.int32, (n_chunks, 2 * STATE_DIM), 0) & (chunks_per_seq - 1)

    def build_slab(taps_ref, slot, c):
        def rows_j(j, carry):
            for e in range(per_slab):
                row = j * n + c * per_slab + e
                taps = jnp.broadcast_to(taps_ref[pl.ds(row, 1), :], (CHUNK, CHUNK))
                shifted = pltpu.roll(taps, 0, 1, stride=1, stride_axis=0)
                t_ref[slot, c, pl.ds(pl.multiple_of(j * CHUNK, CHUNK), CHUNK),
                      e * CHUNK:(e + 1) * CHUNK] = jnp.where(causal, shifted, 0.0).astype(BF16)
            return carry
        lax.fori_loop(0, n, rows_j, 0, unroll=True)

    def shift_rows(x, k):
        return jnp.where(pos >= k, pltpu.roll(x, k, 0), 0.0)

    def group(q, next_taps_ref):
        u = u_ref[q]
        ub = u.astype(BF16)
        b = jnp.dot(ub, w_ref[q], preferred_element_type=F32)
        h = shift_rows(b, 1)
        for k in range(int(math.log2(chunks_per_seq))):
            hs = shift_rows(h, 1 << k)
            h = (h + hs * ap1_ref[q, k:k + 1, :]
                 + pltpu.roll(hs, STATE_DIM, 1) * ap2_ref[q, k:k + 1, :])
        hv = jnp.dot(h.astype(BF16), v_ref[q], preferred_element_type=F32)

        def rebuild(c, carry):
            build_slab(next_taps_ref, 1 - q, c)
            return carry
        lax.fori_loop(0, n_slabs, rebuild, 0, unroll=True)

        pitch = n_chunks + STAGE_PAD
        base = q * n * pitch
        for c in range(n_slabs):
            yc = jnp.dot(ub, t_ref[q, c], preferred_element_type=F32)
            for e in range(per_slab):
                i = c * per_slab + e
                cols = slice(i * CHUNK, (i + 1) * CHUNK)
                yi = yc[:, e * CHUNK:(e + 1) * CHUNK] + hv[:, cols] + d_ref[q, :, cols] * u[:, cols]
                ystage[base + i * pitch:base + i * pitch + n_chunks, :] = jax.nn.gelu(yi)
        for c in range(n_chunks):
            for i0 in range(0, n, 8):
                yt_ref[q * n + i0:q * n + i0 + 8, c * CHUNK:(c + 1) * CHUNK] = (
                    ystage[pl.ds(base + i0 * pitch + c, 8, stride=pitch), :])

    @pl.when(pl.program_id(0) == 0)
    def _():
        def first(c, carry):
            build_slab(krow_ref.at[0], 0, c)
            return carry
        lax.fori_loop(0, n_slabs, first, 0)

    group(0, krow_ref.at[1])
    group(1, knext_ref)


def _s5_main(u_g, krow, w, v, ap1, ap2, d_rep, chunks_per_seq):
    g, n_chunks, width = u_g.shape
    gs = GROUPS_PER_STEP
    assert chunks_per_seq & (chunks_per_seq - 1) == 0 and chunks_per_seq <= 1 << N_DOUBLINGS
    assert g % gs == 0 and gs == 2
    spec = lambda *shape: pl.BlockSpec((gs,) + shape, lambda i: (i,) + (0,) * len(shape))
    taps = (SSM_GROUP * SSM_GROUP, CHUNK)
    return pl.pallas_call(
        functools.partial(_s5_kernel, chunks_per_seq=chunks_per_seq),
        grid=(g // gs,),
        in_specs=[spec(n_chunks, width), spec(*taps),
                  pl.BlockSpec((None,) + taps, lambda i: (jnp.minimum(gs * i + gs, g - 1), 0, 0)),
                  spec(width, 2 * STATE_DIM), spec(2 * STATE_DIM, width),
                  spec(N_DOUBLINGS, 2 * STATE_DIM), spec(N_DOUBLINGS, 2 * STATE_DIM),
                  spec(1, width)],
        out_specs=pl.BlockSpec((gs * SSM_GROUP, n_chunks * CHUNK), lambda i: (i, 0)),
        out_shape=jax.ShapeDtypeStruct((g * SSM_GROUP, n_chunks * CHUNK), F32),
        scratch_shapes=[pltpu.VMEM((gs, width // T_COLS, width, T_COLS), BF16),
                        pltpu.VMEM((gs * SSM_GROUP * (n_chunks + STAGE_PAD), CHUNK), F32)],
        compiler_params=_params(1),
        name="s5_main",
    )(u_g, krow, krow, w, v, ap1, ap2, d_rep)


def _glu_kernel(yt_ref, w_ref, b_ref, g_ref, o_ref):
    y = yt_ref[...]
    z = lax.dot_general(w_ref[...], y.astype(BF16), (((0,), (0,)), ((), ())),
                        preferred_element_type=F32) + b_ref[...]
    ssm = y * jax.nn.sigmoid(z)
    inv = lax.rsqrt(jnp.mean(ssm * ssm, axis=0, keepdims=True) + EPS)
    o_ref[...] = (ssm * inv * g_ref[...]).astype(BF16)


def _glu(yt, w_glu_bf, b_glu, ssm_out_g):
    w, t = yt.shape
    tn = GLU_TOKENS
    return pl.pallas_call(
        _glu_kernel,
        grid=(t // tn,),
        in_specs=[pl.BlockSpec((w, tn), lambda m: (0, m)),
                  pl.BlockSpec((w, w), lambda m: (0, 0)),
                  pl.BlockSpec((w, 1), lambda m: (0, 0)),
                  pl.BlockSpec((w, 1), lambda m: (0, 0))],
        out_specs=pl.BlockSpec((w, tn), lambda m: (0, m)),
        out_shape=jax.ShapeDtypeStruct((w, t), BF16),
        compiler_params=_params(1),
        name="glu",
    )(yt, w_glu_bf, b_glu.reshape(w, 1), ssm_out_g.reshape(w, 1))


OUT_COLS = 512


def _outproj_kernel(attn_ref, ssmt_ref, x_ref, mod_ref, ag_ref, wa_ref, ws_ref, g2_ref,
                    x1_ref, h2_ref):
    d = x_ref.shape[1]
    attn = jnp.concatenate([attn_ref[hd] for hd in range(N_HEADS)], axis=-1)
    attn_n = (_rms(attn.astype(F32)) * ag_ref[...]).astype(BF16)
    ssmt = ssmt_ref[...]
    sq = jnp.zeros((x_ref.shape[0], 1), F32)
    for c0 in range(0, d, OUT_COLS):
        cols = slice(c0, c0 + OUT_COLS)
        mixed = jnp.dot(attn_n, wa_ref[:, cols], preferred_element_type=F32)
        mixed = mixed + lax.dot_general(ssmt, ws_ref[:, cols], (((0,), (0,)), ((), ())),
                                        preferred_element_type=F32)
        x1 = x_ref[:, cols] + mod_ref[0, 2:3, cols] * mixed
        x1_ref[:, cols] = x1
        sq = sq + jnp.sum(x1 * x1, axis=-1, keepdims=True)
    inv = lax.rsqrt(sq * (1.0 / d) + EPS)
    for c0 in range(0, d, OUT_COLS):
        cols = slice(c0, c0 + OUT_COLS)
        gain = g2_ref[:, cols] * (1.0 + mod_ref[0, 4:5, cols])
        h2_ref[:, cols] = (x1_ref[:, cols] * inv * gain + mod_ref[0, 3:4, cols]).astype(BF16)


def _outproj(attn3, ssmt, x2, mod3, attn_out_g, w_out_bf, norm2_g, seq):
    t, d = x2.shape
    wa = ATTN_WIDTH
    ws = ssmt.shape[0]
    tm = OUTPROJ_ROWS
    per_batch = seq // tm
    return pl.pallas_call(
        _outproj_kernel,
        grid=(t // tm,),
        in_specs=[pl.BlockSpec((N_HEADS, tm, HEAD_DIM), lambda m: (0, m, 0)),
                  pl.BlockSpec((ws, tm), lambda m: (0, m)),
                  pl.BlockSpec((tm, d), lambda m: (m, 0)),
                  pl.BlockSpec((1, N_MOD, d), lambda m: (m // per_batch, 0, 0)),
                  pl.BlockSpec((1, wa), lambda m: (0, 0)),
                  pl.BlockSpec((wa, d), lambda m: (0, 0)),
                  pl.BlockSpec((ws, d), lambda m: (1, 0)),
                  pl.BlockSpec((1, d), lambda m: (0, 0))],
        out_specs=[pl.BlockSpec((tm, d), lambda m: (m, 0)),
                   pl.BlockSpec((tm, d), lambda m: (m, 0))],
        out_shape=[jax.ShapeDtypeStruct((t, d), F32),
                   jax.ShapeDtypeStruct((t, d), BF16)],
        compiler_params=_params(1),
        name="outproj",
    )(attn3, ssmt, x2, mod3, attn_out_g.reshape(1, wa), w_out_bf, w_out_bf,
      norm2_g.reshape(1, d))


def _ffn_kernel(h_ref, w1_ref, w2_ref, x1_ref, mod_ref, o_ref):
    f = pl.program_id(1)

    def partial_sum():
        a = jnp.dot(h_ref[...], w1_ref[...], preferred_element_type=F32)
        a = jnp.square(jnp.maximum(a, 0.0)).astype(BF16)
        return jnp.dot(a, w2_ref[...], preferred_element_type=F32)

    @pl.when(f == 0)
    def _():
        o_ref[...] = partial_sum()

    @pl.when(f > 0)
    def _():
        o_ref[...] += partial_sum()

    @pl.when(f == pl.num_programs(1) - 1)
    def _():
        o_ref[...] = x1_ref[...] + mod_ref[0, 5:6, :] * o_ref[...]


def _ffn(h2, w1_bf, w2_bf, x1, mod3, seq):
    t, d = x1.shape
    dff = w1_bf.shape[1]
    tm = FFN_ROWS
    tf = FFN_HIDDEN
    per_batch = seq // tm
    return pl.pallas_call(
        _ffn_kernel,
        grid=(t // tm, dff // tf),
        in_specs=[pl.BlockSpec((tm, d), lambda m, f: (m, 0)),
                  pl.BlockSpec((d, tf), lambda m, f: (0, f)),
                  pl.BlockSpec((tf, d), lambda m, f: (f, 0)),
                  pl.BlockSpec((tm, d), lambda m, f: (m, 0)),
                  pl.BlockSpec((1, N_MOD, d), lambda m, f: (m // per_batch, 0, 0))],
        out_specs=pl.BlockSpec((tm, d), lambda m, f: (m, 0)),
        out_shape=jax.ShapeDtypeStruct((t, d), F32),
        compiler_params=_params(2),
        name="ffn",
    )(h2, w1_bf, w2_bf, x1, mod3)


def _layer(x, c, w_ada, b_ada, norm1_g, w_in, q_norm_g, k_norm_g, lam_re, lam_im, log_step,
           b_re, b_im, c_re, c_im, d_skip, w_glu, b_glu, attn_out_g, ssm_out_g, w_out,
           norm2_g, w_ff1, w_ff2):
    bsz, seq, d = x.shape
    t = bsz * seq
    n_groups = lam_re.shape[0]
    chunks_per_seq = seq // CHUNK

    mod3 = _adaln(c, w_ada, b_ada).reshape(bsz, N_MOD, d)
    x2 = x.reshape(t, d)
    w_qkv = w_in[:, :3 * ATTN_WIDTH].astype(BF16)
    w_u = w_in[:, 3 * ATTN_WIDTH:].astype(BF16)
    qkv, u_g = _inproj(x2, mod3, norm1_g, w_qkv, w_u, q_norm_g, k_norm_g, seq)

    slopes = 2.0 ** (-8.0 * (jnp.arange(N_HEADS, dtype=F32) + 1.0) / N_HEADS)
    slopes = jnp.broadcast_to(slopes[:, None, None], (N_HEADS, 1, 2 * BAND))
    attn = _attention(qkv.reshape(3 * N_HEADS, bsz, seq, HEAD_DIM), slopes)

    krow, w_s, v_s, ap1, ap2 = _s5_params(lam_re, lam_im, log_step, b_re, b_im, c_re, c_im)
    d_rep = jnp.repeat(d_skip.reshape(n_groups, SSM_GROUP), CHUNK, axis=1)
    yt = _s5_main(u_g, krow, w_s, v_s, ap1, ap2, d_rep.reshape(n_groups, 1, SSM_GROUP * CHUNK),
                  chunks_per_seq)
    ssm = _glu(yt, w_glu.astype(BF16), b_glu, ssm_out_g)
    x1, h2 = _outproj(attn.reshape(N_HEADS, t, HEAD_DIM), ssm, x2, mod3, attn_out_g,
                      w_out.astype(BF16), norm2_g, seq)
    out = _ffn(h2, w_ff1.astype(BF16), w_ff2.astype(BF16), x1, mod3, seq)
    return out.reshape(bsz, seq, d)


def kernel(x, c, w_ada, b_ada, norm1_g, w_in, q_norm_g, k_norm_g, lam_re, lam_im, log_step,
           b_re, b_im, c_re, c_im, d_skip, w_glu, b_glu, attn_out_g, ssm_out_g, w_out,
           norm2_g, w_ff1, w_ff2):
    for l in range(w_ada.shape[0]):
        x = _layer(x, c, w_ada[l], b_ada[l], norm1_g[l], w_in[l], q_norm_g[l], k_norm_g[l],
                   lam_re[l], lam_im[l], log_step[l], b_re[l], b_im[l], c_re[l], c_im[l],
                   d_skip[l], w_glu[l], b_glu[l], attn_out_g[l], ssm_out_g[l], w_out[l],
                   norm2_g[l], w_ff1[l], w_ff2[l])
    return x
```

```python
import functools
import math

import jax
import jax.numpy as jnp
from jax import lax
from jax.experimental import pallas as pl
from jax.experimental.pallas import tpu as pltpu

F32 = jnp.float32
BF16 = jnp.bfloat16

HEAD_DIM = 128
N_HEADS = 8
ATTN_WIDTH = N_HEADS * HEAD_DIM
SSM_GROUP = 16
STATE_DIM = 64
CHUNK = 128
BAND = 128
DILATIONS = (1, 4, 16)
N_MOD = 6
EPS = 1e-6
NEG = -1e30
LOG2E = math.log2(math.e)

V7X_VMEM_BYTES = 64 * 1024 * 1024
V7X_SUBLANES = 8
VMEM_LIMIT = V7X_VMEM_BYTES - 8 * 1024 * 1024

ADALN_COLS = 1024
INPROJ_ROWS = 1024
GLU_TOKENS = 1024
OUTPROJ_ROWS = 512
FFN_ROWS = 512
FFN_HIDDEN = 1024


def _params(n_axes):
    return pltpu.CompilerParams(
        dimension_semantics=("arbitrary",) * n_axes, vmem_limit_bytes=VMEM_LIMIT)


def _rms(x, eps=EPS):
    return x * lax.rsqrt(jnp.mean(x * x, axis=-1, keepdims=True) + eps)


def _adaln_kernel(c_ref, w_ref, b_ref, o_ref):
    c = c_ref[...]
    s = (c * jax.nn.sigmoid(c)).astype(BF16)
    o_ref[...] = jnp.dot(s, w_ref[...].astype(BF16), preferred_element_type=F32) + b_ref[...]


def _adaln(c, w_ada, b_ada):
    bsz, d = c.shape
    n_out = w_ada.shape[1]
    rows = V7X_SUBLANES
    tn = ADALN_COLS
    assert bsz <= rows
    c8 = jnp.zeros((rows, d), F32).at[:bsz].set(c)
    out = pl.pallas_call(
        _adaln_kernel,
        grid=(n_out // tn,),
        in_specs=[pl.BlockSpec((rows, d), lambda n: (0, 0)),
                  pl.BlockSpec((d, tn), lambda n: (0, n)),
                  pl.BlockSpec((1, tn), lambda n: (0, n))],
        out_specs=pl.BlockSpec((rows, tn), lambda n: (0, n)),
        out_shape=jax.ShapeDtypeStruct((rows, n_out), F32),
        compiler_params=_params(1),
        name="adaln",
    )(c8, w_ada, b_ada.reshape(1, n_out))
    return out[:bsz]


STAGE_PAD = 4


def _inproj_kernel(x_ref, mod_ref, g1_ref, w_ref, wu_ref, qg_ref, kg_ref, qkv_ref, u_ref,
                   h_ref, ustage):
    n = pl.program_id(1)
    tm = h_ref.shape[0]

    @pl.when(n == 0)
    def _():
        shift = mod_ref[0, 0:1, :]
        scale = mod_ref[0, 1:2, :]
        y = _rms(x_ref[...]) * g1_ref[...]
        h_ref[...] = (y * (1.0 + scale) + shift).astype(BF16)

    def head_norm(branch, gain):
        for pair in range(N_HEADS // 2):
            c0 = pair * 2 * HEAD_DIM
            w0 = branch * ATTN_WIDTH + c0
            acc = jnp.dot(h_ref[...], w_ref[:, w0:w0 + 2 * HEAD_DIM], preferred_element_type=F32)
            for hd in range(2):
                sl = slice(hd * HEAD_DIM, (hd + 1) * HEAD_DIM)
                qkv_ref[2 * pair + hd] = (_rms(acc[:, sl]) * gain).astype(BF16)

    @pl.when(n == 0)
    def _():
        head_norm(0, qg_ref[...] * (HEAD_DIM ** -0.5 * LOG2E))

    @pl.when(n == 1)
    def _():
        head_norm(1, kg_ref[...])

    @pl.when(n == 2)
    def _():
        acc = jnp.dot(h_ref[...], w_ref[:, 2 * ATTN_WIDTH:3 * ATTN_WIDTH],
                      preferred_element_type=F32)
        for hd in range(N_HEADS):
            qkv_ref[hd] = acc[:, hd * HEAD_DIM:(hd + 1) * HEAD_DIM].astype(BF16)

    @pl.when(n == 3)
    def _():
        width = wu_ref.shape[1]
        pitch = width + STAGE_PAD
        sub = 2 * CHUNK
        for c2 in range(tm // sub):
            ut = jnp.dot(h_ref[c2 * sub:(c2 + 1) * sub, :], wu_ref[...],
                         preferred_element_type=F32).T
            for cc in range(sub // CHUNK):
                c = c2 * (sub // CHUNK) + cc
                ustage[c * pitch:c * pitch + width, :] = ut[:, cc * CHUNK:(cc + 1) * CHUNK]

        def regroup(g, carry):
            for j in range(SSM_GROUP):
                u_ref[g, :, j * CHUNK:(j + 1) * CHUNK] = (
                    ustage[pl.ds(g * SSM_GROUP + j, tm // CHUNK, stride=pitch), :])
            return carry
        lax.fori_loop(0, width // SSM_GROUP, regroup, 0)


def _inproj(x2, mod3, norm1_g, w_qkv_bf, w_u_bf, q_norm_g, k_norm_g, seq):
    t, d = x2.shape
    tm = INPROJ_ROWS
    tn = ATTN_WIDTH
    width = w_u_bf.shape[1]
    n_groups = width // SSM_GROUP
    assert w_qkv_bf.shape[1] == 3 * tn and seq % tm == 0 and tm // CHUNK == V7X_SUBLANES
    per_batch = seq // tm
    return pl.pallas_call(
        _inproj_kernel,
        grid=(t // tm, 4),
        in_specs=[pl.BlockSpec((tm, d), lambda m, n: (m, 0)),
                  pl.BlockSpec((1, N_MOD, d), lambda m, n: (m // per_batch, 0, 0)),
                  pl.BlockSpec((1, d), lambda m, n: (0, 0)),
                  pl.BlockSpec((d, 3 * tn), lambda m, n: (0, 0), pipeline_mode=pl.Buffered(1)),
                  pl.BlockSpec((d, width), lambda m, n: (0, 0), pipeline_mode=pl.Buffered(1)),
                  pl.BlockSpec((1, HEAD_DIM), lambda m, n: (0, 0)),
                  pl.BlockSpec((1, HEAD_DIM), lambda m, n: (0, 0))],
        out_specs=[pl.BlockSpec((N_HEADS, tm, HEAD_DIM), lambda m, n: (jnp.minimum(n, 2), m, 0)),
                   pl.BlockSpec((n_groups, tm // CHUNK, SSM_GROUP * CHUNK), lambda m, n: (0, m, 0))],
        out_shape=[jax.ShapeDtypeStruct((3 * N_HEADS, t, HEAD_DIM), BF16),
                   jax.ShapeDtypeStruct((n_groups, t // CHUNK, SSM_GROUP * CHUNK), F32)],
        scratch_shapes=[pltpu.VMEM((tm, d), BF16),
                        pltpu.VMEM((tm // CHUNK * (width + STAGE_PAD), CHUNK), F32)],
        compiler_params=_params(2),
        name="inproj",
    )(x2, mod3, norm1_g.reshape(1, d), w_qkv_bf, w_u_bf, q_norm_g.reshape(1, HEAD_DIM),
      k_norm_g.reshape(1, HEAD_DIM))


ATTN_SUPER = BAND * max(DILATIONS)
BLOCKS_PER_SUPER = ATTN_SUPER // BAND
WIDE_DILATION = 16
WIDE_PITCH = BAND + 8


def _attn_kernel(slope_ref, q_ref, k_ref, v_ref, o_ref,
                 stage, stage4, q4, q16, k1, k4, k16, v1, v4, v16,
                 bias_ref, oacc, mstat, lstat, *, seq):
    assert DILATIONS == (1, 4, 16)
    n_pat = len(DILATIONS)
    n4 = ATTN_SUPER // 4
    n16 = ATTN_SUPER // 16

    def deinterleave(src, a1, a4, a16, pad):
        def chunk(c, carry):
            r0 = pl.multiple_of(c * ATTN_SUPER, ATTN_SUPER)
            x = src[pl.ds(r0, ATTN_SUPER), :]
            if a1 is not None:
                a1[0, pl.ds(pad + r0, ATTN_SUPER), :] = x
            stage[...] = x.astype(F32)
            for r in range(4):
                y = stage[pl.ds(r, n4, stride=4), :]
                stage4[r * n4:(r + 1) * n4, :] = y
                a4[r, pl.ds(pad + pl.multiple_of(c * n4, n4), n4), :] = y.astype(BF16)
            for r in range(4):
                for a in range(4):
                    z = stage4[pl.ds(r * n4 + a, n16, stride=4), :]
                    a16[4 * a + r, pl.ds(pad + pl.multiple_of(c * n16, n16), n16), :] = (
                        z.astype(BF16))
            return carry
        lax.fori_loop(0, seq // ATTN_SUPER, chunk, 0)

    for buf in (k1, k4, k16, v1, v4, v16):
        buf[:, 0:BAND, :] = jnp.zeros((buf.shape[0], BAND, HEAD_DIM), BF16)
    deinterleave(q_ref, None, q4, q16, 0)
    deinterleave(k_ref, k1, k4, k16, BAND)
    deinterleave(v_ref, v1, v4, v16, BAND)

    qi = lax.broadcasted_iota(jnp.int32, (BAND, 2 * BAND), 0)
    ki = lax.broadcasted_iota(jnp.int32, (BAND, 2 * BAND), 1)
    steps = qi - ki + BAND
    valid = (steps >= 0) & (steps <= BAND)
    slope = slope_ref[...]
    for p, d in enumerate(DILATIONS):
        bias = jnp.where(valid, -(slope * (float(d) * LOG2E)) * steps.astype(F32), NEG)
        bias_ref[2 * p + 1] = bias
        bias_ref[2 * p] = jnp.where(ki >= BAND, bias, NEG)

    q_streams = (None, q4, q16)
    k_streams = (k1, k4, k16)
    v_streams = (v1, v4, v16)
    ones = jnp.ones((2 * BAND, HEAD_DIM), BF16)

    def rows(start, size, d):
        return pl.ds(start, size) if d == 1 else pl.ds(start, size, stride=d)

    def block(p, d, r, j, local):
        row0 = pl.multiple_of(j * BAND, BAND)
        if d == 1:
            q = q_ref[pl.ds(row0, BAND), :]
        else:
            q = q_streams[p][r, pl.ds(row0, BAND), :]
        k2 = k_streams[p][r, pl.ds(row0, 2 * BAND), :]
        v2 = v_streams[p][r, pl.ds(row0, 2 * BAND), :]
        s = lax.dot_general(q, k2, (((1,), (1,)), ((), ())), preferred_element_type=F32)
        has_prev = jnp.minimum(j, 1)
        s = s + bias_ref[2 * p + has_prev]
        m = jnp.max(s, axis=-1, keepdims=True)
        e = jnp.exp2(s - m).astype(BF16)
        acc = jnp.dot(e, jnp.concatenate([v2, ones], axis=1), preferred_element_type=F32)
        if d == WIDE_DILATION:
            dst = pl.ds(pl.multiple_of(r * WIDE_PITCH, 8), BAND)
        else:
            dst = rows(local, BAND, d)
        oacc[p, dst, :] = acc[:, :HEAD_DIM]
        lstat[p, dst, :] = acc[:, HEAD_DIM:]
        mstat[p, dst, :] = jnp.broadcast_to(m, (BAND, HEAD_DIM))

    merge_rows = 256

    def super_block(sb, carry):
        p0 = sb * ATTN_SUPER

        def blocks(idx, c2):
            for p, d in enumerate(DILATIONS):
                r = idx & (d - 1)
                jl = idx >> int(math.log2(d))
                block(p, d, r, sb * (BLOCKS_PER_SUPER // d) + jl, r + BAND * d * jl)
            return c2
        lax.fori_loop(0, BLOCKS_PER_SUPER, blocks, 0, unroll=True)

        def merge(c, c2):
            r = pl.multiple_of(c * merge_rows, merge_rows)

            def natural(buf, p):
                if DILATIONS[p] != WIDE_DILATION:
                    return buf[p, pl.ds(r, merge_rows), :]
                i0 = c * (merge_rows // WIDE_DILATION)
                return jnp.concatenate(
                    [buf[p, pl.ds(i0 + i + half * 8 * WIDE_PITCH, 8, stride=WIDE_PITCH), :]
                     for i in range(merge_rows // WIDE_DILATION)
                     for half in range(WIDE_DILATION // 8)], axis=0)

            ms = [natural(mstat, p) for p in range(n_pat)]
            mx = functools.reduce(jnp.maximum, ms)
            num = jnp.zeros((merge_rows, HEAD_DIM), F32)
            den = jnp.zeros((merge_rows, HEAD_DIM), F32)
            for p in range(n_pat):
                w = jnp.exp2(ms[p] - mx)
                num = num + w * natural(oacc, p)
                den = den + w * natural(lstat, p)
            o_ref[pl.ds(pl.multiple_of(p0 + r, merge_rows), merge_rows), :] = (num / den).astype(BF16)
            return c2
        lax.fori_loop(0, ATTN_SUPER // merge_rows, merge, 0)
        return carry
    lax.fori_loop(0, seq // ATTN_SUPER, super_block, 0)


def _attention(qkv4, slopes):
    _, bsz, seq, _ = qkv4.shape
    assert seq % ATTN_SUPER == 0
    blk = lambda off: pl.BlockSpec((None, None, seq, HEAD_DIM), lambda b, h: (off + h, b, 0, 0))
    stream = lambda d, pad: pltpu.VMEM((d, pad + seq // d, HEAD_DIM), BF16)
    stat = pltpu.VMEM((len(DILATIONS), WIDE_DILATION * WIDE_PITCH, HEAD_DIM), F32)
    return pl.pallas_call(
        functools.partial(_attn_kernel, seq=seq),
        grid=(bsz, N_HEADS),
        in_specs=[pl.BlockSpec((None, 1, 2 * BAND), lambda b, h: (h, 0, 0)),
                  blk(0), blk(N_HEADS), blk(2 * N_HEADS)],
        out_specs=pl.BlockSpec((None, None, seq, HEAD_DIM), lambda b, h: (h, b, 0, 0)),
        out_shape=jax.ShapeDtypeStruct((N_HEADS, bsz, seq, HEAD_DIM), BF16),
        scratch_shapes=[pltpu.VMEM((ATTN_SUPER, HEAD_DIM), F32),
                        pltpu.VMEM((ATTN_SUPER, HEAD_DIM), F32),
                        stream(4, 0), stream(16, 0),
                        stream(1, BAND), stream(4, BAND), stream(16, BAND),
                        stream(1, BAND), stream(4, BAND), stream(16, BAND),
                        pltpu.VMEM((2 * len(DILATIONS), BAND, 2 * BAND), F32),
                        stat, stat, stat],
        compiler_params=_params(2),
        name="attention",
    )(slopes, qkv4, qkv4, qkv4)


N_DOUBLINGS = 8


def _cmul(ar, ai, br, bi):
    return ar * br - ai * bi, ar * bi + ai * br


def _cpow(ar, ai, e, nbits):
    shape = jnp.broadcast_shapes(ar.shape, e.shape)
    pr = jnp.ones(shape, F32)
    pi = jnp.zeros(shape, F32)
    br, bi = ar, ai
    for k in range(nbits):
        nr, ni = _cmul(pr, pi, br, bi)
        sel = ((e >> k) & 1) == 1
        pr = jnp.where(sel, nr, pr)
        pi = jnp.where(sel, ni, pi)
        br, bi = _cmul(br, bi, br, bi)
    return pr, pi


def _zoh(lr, li, log_step):
    dt = jnp.exp(log_step)
    mag = jnp.exp(lr * dt)
    ar = mag * jnp.cos(li * dt)
    ai = mag * jnp.sin(li * dt)
    den = lr * lr + li * li
    cr = ((ar - 1.0) * lr + ai * li) / den
    ci = (ai * lr - (ar - 1.0) * li) / den
    return ar, ai, cr, ci


def _s5_params_kernel(lr_row, li_row, ls_row, lr_col, li_col, ls_col,
                      bt_re, bt_im, c_re, c_im, ct_re, ct_im,
                      krow_ref, w_ref, v_ref, ap1_ref, ap2_ref):
    p_dim = STATE_DIM
    ar, ai, cr, ci = _zoh(lr_row[...], li_row[...], ls_row[...])
    bbr, bbi = _cmul(cr, ci, bt_re[...], bt_im[...])
    s_idx = lax.broadcasted_iota(jnp.int32, (CHUNK, p_dim), 0)
    qr, qi = _cpow(ar, ai, CHUNK - 1 - s_idx, 7)
    for j in range(SSM_GROUP):
        wr, wi = _cmul(qr, qi, bbr[j:j + 1, :], bbi[j:j + 1, :])
        w_ref[j * CHUNK:(j + 1) * CHUNK, :] = jnp.concatenate([wr, wi], axis=-1).astype(BF16)
    mr, mi = ar, ai
    for _ in range(7):
        mr, mi = _cmul(mr, mi, mr, mi)
    for k in range(N_DOUBLINGS):
        ap1_ref[k:k + 1, :] = jnp.concatenate([mr, mr], axis=-1)
        ap2_ref[k:k + 1, :] = jnp.concatenate([-mi, mi], axis=-1)
        mr, mi = _cmul(mr, mi, mr, mi)

    acr, aci, _, _ = _zoh(lr_col[...], li_col[...], ls_col[...])
    t_idx = lax.broadcasted_iota(jnp.int32, (p_dim, CHUNK), 1)
    pr, pi = _cpow(acr, aci, t_idx, 7)
    cbr, cbi = [], []
    for j in range(SSM_GROUP):
        r_, i_ = _cmul(c_re[...], c_im[...], bbr[j:j + 1, :], bbi[j:j + 1, :])
        cbr.append(r_)
        cbi.append(i_)
    cbr = jnp.concatenate(cbr, axis=0)
    cbi = jnp.concatenate(cbi, axis=0)
    hi = lax.Precision.HIGHEST
    krow_ref[...] = (jnp.dot(cbr, pr, precision=hi, preferred_element_type=F32)
                     - jnp.dot(cbi, pi, precision=hi, preferred_element_type=F32))
    p1r, p1i = _cmul(pr, pi, acr, aci)
    ctr = ct_re[...]
    cti = ct_im[...]
    for i in range(SSM_GROUP):
        vr, vi = _cmul(ctr[:, i:i + 1], cti[:, i:i + 1], p1r, p1i)
        v_ref[0:p_dim, i * CHUNK:(i + 1) * CHUNK] = vr.astype(BF16)
        v_ref[p_dim:2 * p_dim, i * CHUNK:(i + 1) * CHUNK] = (-vi).astype(BF16)


def _s5_params(lam_re, lam_im, log_step, b_re, b_im, c_re, c_im):
    g, p = lam_re.shape
    n = SSM_GROUP
    row = lambda a: a.reshape(g, 1, p)
    col = lambda a: a.reshape(g, p, 1)
    ls = jnp.broadcast_to(log_step[:, None], (g, p))
    tr = lambda a: jnp.swapaxes(a, 1, 2)
    spec = lambda *shape: pl.BlockSpec((None,) + shape, lambda i: (i,) + (0,) * len(shape))
    return pl.pallas_call(
        _s5_params_kernel,
        grid=(g,),
        in_specs=[spec(1, p)] * 3 + [spec(p, 1)] * 3 + [spec(n, p)] * 4 + [spec(p, n)] * 2,
        out_specs=[spec(n * n, CHUNK), spec(n * CHUNK, 2 * p), spec(2 * p, n * CHUNK),
                   spec(N_DOUBLINGS, 2 * p), spec(N_DOUBLINGS, 2 * p)],
        out_shape=[jax.ShapeDtypeStruct((g, n * n, CHUNK), F32),
                   jax.ShapeDtypeStruct((g, n * CHUNK, 2 * p), BF16),
                   jax.ShapeDtypeStruct((g, 2 * p, n * CHUNK), BF16),
                   jax.ShapeDtypeStruct((g, N_DOUBLINGS, 2 * p), F32),
                   jax.ShapeDtypeStruct((g, N_DOUBLINGS, 2 * p), F32)],
        compiler_params=_params(1),
        name="s5_params",
    )(row(lam_re), row(lam_im), row(ls), col(lam_re), col(lam_im), col(ls),
      tr(b_re), tr(b_im), c_re, c_im, tr(c_re), tr(c_im))


GROUPS_PER_STEP = 2
T_COLS = 4 * CHUNK


def _s5_kernel(u_ref, krow_ref, knext_ref, w_ref, v_ref, ap1_ref, ap2_ref, d_ref, yt_ref,
               t_ref, ystage, *, chunks_per_seq):
    n = SSM_GROUP
    n_chunks = u_ref.shape[1]
    n_slabs = n * CHUNK // T_COLS
    per_slab = T_COLS // CHUNK

    s_idx = lax.broadcasted_iota(jnp.int32, (CHUNK, CHUNK), 0)
    t_idx = lax.broadcasted_iota(jnp.int32, (CHUNK, CHUNK), 1)
    causal = t_idx >= s_idx
    pos = lax.broadcasted_iota(jnp.int32, (n_chunks, 2 * STATE_DIM), 0) & (chunks_per_seq - 1)

    def build_slab(taps_ref, slot, c):
        def rows_j(j, carry):
            for e in range(per_slab):
                row = j * n + c * per_slab + e
                taps = jnp.broadcast_to(taps_ref[pl.ds(row, 1), :], (CHUNK, CHUNK))
                shifted = pltpu.roll(taps, 0, 1, stride=1, stride_axis=0)
                t_ref[slot, c, pl.ds(pl.multiple_of(j * CHUNK, CHUNK), CHUNK),
                      e * CHUNK:(e + 1) * CHUNK] = jnp.where(causal, shifted, 0.0).astype(BF16)
            return carry
        lax.fori_loop(0, n, rows_j, 0, unroll=True)

    def shift_rows(x, k):
        return jnp.where(pos >= k, pltpu.roll(x, k, 0), 0.0)

    def group(q, next_taps_ref):
        u = u_ref[q]
        ub = u.astype(BF16)
        b = jnp.dot(ub, w_ref[q], preferred_element_type=F32)
        h = shift_rows(b, 1)
        for k in range(int(math.log2(chunks_per_seq))):
            hs = shift_rows(h, 1 << k)
            h = (h + hs * ap1_ref[q, k:k + 1, :]
                 + pltpu.roll(hs, STATE_DIM, 1) * ap2_ref[q, k:k + 1, :])
        hv = jnp.dot(h.astype(BF16), v_ref[q], preferred_element_type=F32)

        def rebuild(c, carry):
            build_slab(next_taps_ref, 1 - q, c)
            return carry
        lax.fori_loop(0, n_slabs, rebuild, 0, unroll=True)

        pitch = n_chunks + STAGE_PAD
        base = q * n * pitch
        for c in range(n_slabs):
            yc = jnp.dot(ub, t_ref[q, c], preferred_element_type=F32)
            for e in range(per_slab):
                i = c * per_slab + e
                cols = slice(i * CHUNK, (i + 1) * CHUNK)
                yi = yc[:, e * CHUNK:(e + 1) * CHUNK] + hv[:, cols] + d_ref[q, :, cols] * u[:, cols]
                ystage[base + i * pitch:base + i * pitch + n_chunks, :] = jax.nn.gelu(yi)
        for c in range(n_chunks):
            for i0 in range(0, n, 8):
                yt_ref[q * n + i0:q * n + i0 + 8, c * CHUNK:(c + 1) * CHUNK] = (
                    ystage[pl.ds(base + i0 * pitch + c, 8, stride=pitch), :])

    @pl.when(pl.program_id(0) == 0)
    def _():
        def first(c, carry):
            build_slab(krow_ref.at[0], 0, c)
            return carry
        lax.fori_loop(0, n_slabs, first, 0)

    group(0, krow_ref.at[1])
    group(1, knext_ref)


def _s5_main(u_g, krow, w, v, ap1, ap2, d_rep, chunks_per_seq):
    g, n_chunks, width = u_g.shape
    gs = GROUPS_PER_STEP
    assert chunks_per_seq & (chunks_per_seq - 1) == 0 and chunks_per_seq <= 1 << N_DOUBLINGS
    assert g % gs == 0 and gs == 2
    spec = lambda *shape: pl.BlockSpec((gs,) + shape, lambda i: (i,) + (0,) * len(shape))
    taps = (SSM_GROUP * SSM_GROUP, CHUNK)
    return pl.pallas_call(
        functools.partial(_s5_kernel, chunks_per_seq=chunks_per_seq),
        grid=(g // gs,),
        in_specs=[spec(n_chunks, width), spec(*taps),
                  pl.BlockSpec((None,) + taps, lambda i: (jnp.minimum(gs * i + gs, g - 1), 0, 0)),
                  spec(width, 2 * STATE_DIM), spec(2 * STATE_DIM, width),
                  spec(N_DOUBLINGS, 2 * STATE_DIM), spec(N_DOUBLINGS, 2 * STATE_DIM),
                  spec(1, width)],
        out_specs=pl.BlockSpec((gs * SSM_GROUP, n_chunks * CHUNK), lambda i: (i, 0)),
        out_shape=jax.ShapeDtypeStruct((g * SSM_GROUP, n_chunks * CHUNK), F32),
        scratch_shapes=[pltpu.VMEM((gs, width // T_COLS, width, T_COLS), BF16),
                        pltpu.VMEM((gs * SSM_GROUP * (n_chunks + STAGE_PAD), CHUNK), F32)],
        compiler_params=_params(1),
        name="s5_main",
    )(u_g, krow, krow, w, v, ap1, ap2, d_rep)


def _glu_kernel(yt_ref, w_ref, b_ref, g_ref, o_ref):
    y = yt_ref[...]
    z = lax.dot_general(w_ref[...], y.astype(BF16), (((0,), (0,)), ((), ())),
                        preferred_element_type=F32) + b_ref[...]
    ssm = y * jax.nn.sigmoid(z)
    inv = lax.rsqrt(jnp.mean(ssm * ssm, axis=0, keepdims=True) + EPS)
    o_ref[...] = (ssm * inv * g_ref[...]).astype(BF16)


def _glu(yt, w_glu_bf, b_glu, ssm_out_g):
    w, t = yt.shape
    tn = GLU_TOKENS
    return pl.pallas_call(
        _glu_kernel,
        grid=(t // tn,),
        in_specs=[pl.BlockSpec((w, tn), lambda m: (0, m)),
                  pl.BlockSpec((w, w), lambda m: (0, 0)),
                  pl.BlockSpec((w, 1), lambda m: (0, 0)),
                  pl.BlockSpec((w, 1), lambda m: (0, 0))],
        out_specs=pl.BlockSpec((w, tn), lambda m: (0, m)),
        out_shape=jax.ShapeDtypeStruct((w, t), BF16),
        compiler_params=_params(1),
        name="glu",
    )(yt, w_glu_bf, b_glu.reshape(w, 1), ssm_out_g.reshape(w, 1))


OUT_COLS = 512


def _outproj_kernel(attn_ref, ssmt_ref, x_ref, mod_ref, ag_ref, wa_ref, ws_ref, x1_ref):
    d = x_ref.shape[1]
    attn = jnp.concatenate([attn_ref[hd] for hd in range(N_HEADS)], axis=-1)
    attn_n = (_rms(attn.astype(F32)) * ag_ref[...]).astype(BF16)
    ssmt = ssmt_ref[...]
    for c0 in range(0, d, OUT_COLS):
        cols = slice(c0, c0 + OUT_COLS)
        mixed = jnp.dot(attn_n, wa_ref[:, cols], preferred_element_type=F32)
        mixed = mixed + lax.dot_general(ssmt, ws_ref[:, cols], (((0,), (0,)), ((), ())),
                                        preferred_element_type=F32)
        x1_ref[:, cols] = x_ref[:, cols] + mod_ref[0, 2:3, cols] * mixed


def _outproj(attn3, ssmt, x2, mod3, attn_out_g, w_out_bf, seq):
    t, d = x2.shape
    wa = ATTN_WIDTH
    ws = ssmt.shape[0]
    tm = OUTPROJ_ROWS
    per_batch = seq // tm
    return pl.pallas_call(
        _outproj_kernel,
        grid=(t // tm,),
        in_specs=[pl.BlockSpec((N_HEADS, tm, HEAD_DIM), lambda m: (0, m, 0)),
                  pl.BlockSpec((ws, tm), lambda m: (0, m)),
                  pl.BlockSpec((tm, d), lambda m: (m, 0)),
                  pl.BlockSpec((1, N_MOD, d), lambda m: (m // per_batch, 0, 0)),
                  pl.BlockSpec((1, wa), lambda m: (0, 0)),
                  pl.BlockSpec((wa, d), lambda m: (0, 0)),
                  pl.BlockSpec((ws, d), lambda m: (1, 0))],
        out_specs=pl.BlockSpec((tm, d), lambda m: (m, 0)),
        out_shape=jax.ShapeDtypeStruct((t, d), F32),
        compiler_params=_params(1),
        name="outproj",
    )(attn3, ssmt, x2, mod3, attn_out_g.reshape(1, wa), w_out_bf, w_out_bf)


def _ffn_kernel(g2_ref, w1_ref, w2_ref, x1_ref, mod_ref, o_ref, h_ref):
    f = pl.program_id(1)

    @pl.when(f == 0)
    def _():
        gain = g2_ref[...] * (1.0 + mod_ref[0, 4:5, :])
        h_ref[...] = (_rms(x1_ref[...]) * gain + mod_ref[0, 3:4, :]).astype(BF16)

    def partial_sum():
        a = jnp.dot(h_ref[...], w1_ref[...], preferred_element_type=F32)
        a = jnp.square(jnp.maximum(a, 0.0)).astype(BF16)
        return jnp.dot(a, w2_ref[...], preferred_element_type=F32)

    @pl.when(f == 0)
    def _():
        o_ref[...] = partial_sum()

    @pl.when(f > 0)
    def _():
        o_ref[...] += partial_sum()

    @pl.when(f == pl.num_programs(1) - 1)
    def _():
        o_ref[...] = x1_ref[...] + mod_ref[0, 5:6, :] * o_ref[...]


def _ffn(norm2_g, w1_bf, w2_bf, x1, mod3, seq):
    t, d = x1.shape
    dff = w1_bf.shape[1]
    tm = FFN_ROWS
    tf = FFN_HIDDEN
    per_batch = seq // tm
    return pl.pallas_call(
        _ffn_kernel,
        grid=(t // tm, dff // tf),
        in_specs=[pl.BlockSpec((1, d), lambda m, f: (0, 0)),
                  pl.BlockSpec((d, tf), lambda m, f: (0, f)),
                  pl.BlockSpec((tf, d), lambda m, f: (f, 0)),
                  pl.BlockSpec((tm, d), lambda m, f: (m, 0)),
                  pl.BlockSpec((1, N_MOD, d), lambda m, f: (m // per_batch, 0, 0))],
        out_specs=pl.BlockSpec((tm, d), lambda m, f: (m, 0)),
        out_shape=jax.ShapeDtypeStruct((t, d), F32),
        scratch_shapes=[pltpu.VMEM((tm, d), BF16)],
        compiler_params=_params(2),
        name="ffn",
    )(norm2_g.reshape(1, d), w1_bf, w2_bf, x1, mod3)


def _layer(x, c, w_ada, b_ada, norm1_g, w_in, q_norm_g, k_norm_g, lam_re, lam_im, log_step,
           b_re, b_im, c_re, c_im, d_skip, w_glu, b_glu, attn_out_g, ssm_out_g, w_out,
           norm2_g, w_ff1, w_ff2):
    bsz, seq, d = x.shape
    t = bsz * seq
    n_groups = lam_re.shape[0]
    chunks_per_seq = seq // CHUNK

    mod3 = _adaln(c, w_ada, b_ada).reshape(bsz, N_MOD, d)
    x2 = x.reshape(t, d)
    w_qkv = w_in[:, :3 * ATTN_WIDTH].astype(BF16)
    w_u = w_in[:, 3 * ATTN_WIDTH:].astype(BF16)
    qkv, u_g = _inproj(x2, mod3, norm1_g, w_qkv, w_u, q_norm_g, k_norm_g, seq)

    slopes = 2.0 ** (-8.0 * (jnp.arange(N_HEADS, dtype=F32) + 1.0) / N_HEADS)
    slopes = jnp.broadcast_to(slopes[:, None, None], (N_HEADS, 1, 2 * BAND))
    attn = _attention(qkv.reshape(3 * N_HEADS, bsz, seq, HEAD_DIM), slopes)

    krow, w_s, v_s, ap1, ap2 = _s5_params(lam_re, lam_im, log_step, b_re, b_im, c_re, c_im)
    d_rep = jnp.repeat(d_skip.reshape(n_groups, SSM_GROUP), CHUNK, axis=1)
    yt = _s5_main(u_g, krow, w_s, v_s, ap1, ap2, d_rep.reshape(n_groups, 1, SSM_GROUP * CHUNK),
                  chunks_per_seq)
    ssm = _glu(yt, w_glu.astype(BF16), b_glu, ssm_out_g)
    x1 = _outproj(attn.reshape(N_HEADS, t, HEAD_DIM), ssm, x2, mod3, attn_out_g,
                  w_out.astype(BF16), seq)
    out = _ffn(norm2_g, w_ff1.astype(BF16), w_ff2.astype(BF16), x1, mod3, seq)
    return out.reshape(bsz, seq, d)


def kernel(x, c, w_ada, b_ada, norm1_g, w_in, q_norm_g, k_norm_g, lam_re, lam_im, log_step,
           b_re, b_im, c_re, c_im, d_skip, w_glu, b_glu, attn_out_g, ssm_out_g, w_out,
           norm2_g, w_ff1, w_ff2):
    for l in range(w_ada.shape[0]):
        x = _layer(x, c, w_ada[l], b_ada[l], norm1_g[l], w_in[l], q_norm_g[l], k_norm_g[l],
                   lam_re[l], lam_im[l], log_step[l], b_re[l], b_im[l], c_re[l], c_im[l],
                   d_skip[l], w_glu[l], b_glu[l], attn_out_g[l], ssm_out_g[l], w_out[l],
                   norm2_g[l], w_ff1[l], w_ff2[l])
    return x
```
